```python
import jax
import jax.numpy as jnp
from jax import lax
import numpy as np

D_MODEL = 1024
BATCH = 8
SEQ = 2048
DEPTH = 1
DEC_BATCH = 128
DEC_SEQ = 1
PAST_LEN = 16384
PAGE_SIZE = 128

MIX_WIDTH = D_MODEL
POOL_WIDTH = MIX_WIDTH // 2
POOL_WINDOWS = (2, 4, 8, 16)
POOL_GROUPS = len(POOL_WINDOWS)
POOL_GROUP_WIDTH = POOL_WIDTH // POOL_GROUPS
POOL_BUF = max(POOL_WINDOWS) - 1
GLA_WIDTH = MIX_WIDTH - POOL_WIDTH
GLA_HEADS = 4
GLA_DV = GLA_WIDTH // GLA_HEADS
GLA_DK = GLA_DV // 2
GLA_KEY_WIDTH = GLA_HEADS * GLA_DK
GLA_GATE_RANK = 16
GLA_GATE_NORM = 16.0
GLA_CHUNK = 64
MEM_TOKENS = 256
MEM_HEADS = 4
MEM_HEAD_DIM = D_MODEL // MEM_HEADS
D_FF = 4 * D_MODEL
EPS = 1e-6
SPLITS = (POOL_WIDTH, POOL_WIDTH + GLA_KEY_WIDTH, POOL_WIDTH + 2 * GLA_KEY_WIDTH,
          POOL_WIDTH + 2 * GLA_KEY_WIDTH + GLA_WIDTH, POOL_WIDTH + 2 * GLA_KEY_WIDTH + 2 * GLA_WIDTH)
IN_COLS = POOL_WIDTH + 2 * GLA_KEY_WIDTH + 2 * GLA_WIDTH + GLA_GATE_RANK

kernel_name = 'hybrid_pool_gla_memxattn_step'


def _rmsnorm(x, g):
    xf = x.astype(jnp.float32)
    r = xf * lax.rsqrt(jnp.mean(xf * xf, axis=-1, keepdims=True) + EPS)
    return (r * g.astype(jnp.float32)).astype(x.dtype)


def _pool_mixer(u_ext, n_prefix, w_pool, pool_scale):
    b, length, _ = u_ext.shape
    uf = u_ext.astype(jnp.float32)
    cs = jnp.concatenate([jnp.zeros((b, 1, POOL_WIDTH), jnp.float32), jnp.cumsum(uf, axis=1)], axis=1)
    idx = jnp.arange(n_prefix, length)
    u_new = uf[:, n_prefix:]
    outs = []
    for gi, w in enumerate(POOL_WINDOWS):
        c0 = gi * POOL_GROUP_WIDTH
        c1 = c0 + POOL_GROUP_WIDTH
        lo = jnp.maximum(idx + 1 - w, 0)
        win_sum = cs[:, idx + 1, c0:c1] - cs[:, lo, c0:c1]
        count = (idx + 1 - lo).astype(jnp.float32)[None, :, None]
        pooled = win_sum / count - u_new[:, :, c0:c1]
        outs.append(jnp.einsum('btc,cd->btd', pooled, w_pool[gi].astype(jnp.float32)))
    y = jnp.concatenate(outs, axis=-1) * pool_scale.astype(jnp.float32)
    return y.astype(u_ext.dtype)


def _to_chunks(a, n_chunks, chunk):
    b, t, h, d = a.shape
    pad = n_chunks * chunk - t
    a = jnp.pad(a.astype(jnp.float32), ((0, 0), (0, pad), (0, 0), (0, 0)))
    a = a.reshape(b, n_chunks, chunk, h, d)
    return jnp.transpose(a, (1, 0, 3, 2, 4))


def _gla(q, k, v, log_f, s0):
    b, t = q.shape[:2]
    chunk = min(GLA_CHUNK, t)
    n_chunks = -(-t // chunk)
    qc, kc, vc, gc = (_to_chunks(a, n_chunks, chunk) for a in (q, k, v, log_f))
    cum = jnp.cumsum(gc, axis=3)
    cum_last = cum[:, :, :, -1:, :]
    q_dec = qc * jnp.exp(cum)
    k_inv = kc * jnp.exp(-cum)
    k_end = kc * jnp.exp(cum_last - cum)
    causal = jnp.tril(jnp.ones((chunk, chunk), jnp.float32))
    scores = jnp.einsum('nbhid,nbhjd->nbhij', q_dec, k_inv) * causal
    o_intra = jnp.einsum('nbhij,nbhjv->nbhiv', scores, vc)

    def step(state, inp):
        q_c, k_c, v_c, last_c = inp
        o_c = jnp.einsum('bhid,bhdv->bhiv', q_c, state)
        state = jnp.exp(last_c[:, :, 0, :])[..., None] * state + jnp.einsum('bhjd,bhjv->bhdv', k_c, v_c)
        return state, o_c

    s_fin, o_inter = lax.scan(step, s0.astype(jnp.float32), (q_dec, k_end, vc, cum_last))
    o = jnp.transpose(o_intra + o_inter, (1, 0, 3, 2, 4)).reshape(b, n_chunks * chunk, GLA_HEADS, GLA_DV)
    return o[:, :t], s_fin


def _mem_kv(mem, mem_norm_g, w_km, w_vm):
    b = mem.shape[0]
    mn = _rmsnorm(mem, mem_norm_g)
    mk = jnp.einsum('bmd,dc->bmc', mn, w_km).reshape(b, MEM_TOKENS, MEM_HEADS, MEM_HEAD_DIM)
    mv = jnp.einsum('bmd,dc->bmc', mn, w_vm).reshape(b, MEM_TOKENS, MEM_HEADS, MEM_HEAD_DIM)
    return mk, mv


def _layer(x, pool_prefix, gla_state, mem_k, mem_v, norm_mix_g, w_in, w_forget_up, b_forget, w_pool,
           pool_scale, gla_norm_g, w_out, norm_mem_g, w_qm, w_om, norm_ffn_g, w_up, w_down):
    b, t, _ = x.shape
    h = _rmsnorm(x, norm_mix_g)
    proj = jnp.einsum('btd,dc->btc', h, w_in)
    u, q, k, v, gate, f_low = jnp.split(proj, SPLITS, axis=-1)
    u_ext = jnp.concatenate([pool_prefix.astype(u.dtype), u], axis=1)
    pool_out = _pool_mixer(u_ext, pool_prefix.shape[1], w_pool, pool_scale)
    new_pool = u_ext[:, -POOL_BUF:]
    log_f = jax.nn.log_sigmoid((jnp.einsum('btr,rk->btk', f_low, w_forget_up) + b_forget).astype(jnp.float32)) / GLA_GATE_NORM
    q = q.reshape(b, t, GLA_HEADS, GLA_DK) * (GLA_DK ** -0.5)
    k = k.reshape(b, t, GLA_HEADS, GLA_DK)
    v = v.reshape(b, t, GLA_HEADS, GLA_DV)
    log_f = log_f.reshape(b, t, GLA_HEADS, GLA_DK)
    o, new_gla = _gla(q, k, v, log_f, gla_state)
    o = _rmsnorm(o, gla_norm_g).reshape(b, t, GLA_WIDTH).astype(x.dtype) * jax.nn.silu(gate)
    mixed = jnp.concatenate([pool_out, o], axis=-1)
    x = x + jnp.einsum('btc,cd->btd', mixed, w_out)
    hm = _rmsnorm(x, norm_mem_g)
    qm = jnp.einsum('btd,dc->btc', hm, w_qm).reshape(b, t, MEM_HEADS, MEM_HEAD_DIM)
    s = jnp.einsum('bthd,bmhd->bhtm', qm.astype(jnp.float32), mem_k.astype(jnp.float32)) * (MEM_HEAD_DIM ** -0.5)
    p = jax.nn.softmax(s, axis=-1)
    ctx = jnp.einsum('bhtm,bmhd->bthd', p, mem_v.astype(jnp.float32)).reshape(b, t, D_MODEL).astype(x.dtype)
    x = x + jnp.einsum('btc,cd->btd', ctx, w_om)
    hf = _rmsnorm(x, norm_ffn_g)
    a = jax.nn.relu(jnp.einsum('btd,df->btf', hf, w_up))
    x = x + jnp.einsum('btf,fd->btd', a * a, w_down)
    return x, new_pool, new_gla.astype(gla_state.dtype)


def setup_inputs(seed: int = 0) -> dict:
    key = jax.random.key(seed)
    ks = jax.random.split(key, 32)
    f32 = jnp.float32

    def nrm(k, shape, scale):
        return jax.random.normal(k, shape, f32) * scale

    def gain(k, shape):
        return 1.0 + 0.02 * jax.random.normal(k, shape, f32)

    L = DEPTH
    return {
        'x_prompt': nrm(ks[0], (BATCH, SEQ, D_MODEL), 1.0),
        'x_sample': nrm(ks[1], (DEC_BATCH, DEC_SEQ, D_MODEL), 1.0),
        'state_pool': nrm(ks[2], (L, DEC_BATCH, POOL_BUF, POOL_WIDTH), 1.0),
        'state_gla': nrm(ks[3], (L, DEC_BATCH, GLA_HEADS, GLA_DK, GLA_DV), 0.5),
        'cache_mem_k': nrm(ks[4], (L, DEC_BATCH, MEM_TOKENS, MEM_HEADS, MEM_HEAD_DIM), 1.0),
        'cache_mem_v': nrm(ks[5], (L, DEC_BATCH, MEM_TOKENS, MEM_HEADS, MEM_HEAD_DIM), 1.0),
        'mem_prompt': nrm(ks[6], (BATCH, MEM_TOKENS, D_MODEL), 1.0),
        'norm_mix_g': gain(ks[7], (L, D_MODEL)),
        'w_in': nrm(ks[8], (L, D_MODEL, IN_COLS), D_MODEL ** -0.5),
        'w_forget_up': nrm(ks[9], (L, GLA_GATE_RANK, GLA_KEY_WIDTH), GLA_GATE_RANK ** -0.5),
        'b_forget': nrm(ks[10], (L, GLA_KEY_WIDTH), 0.1),
        'w_pool': nrm(ks[11], (L, POOL_GROUPS, POOL_GROUP_WIDTH, POOL_GROUP_WIDTH), POOL_GROUP_WIDTH ** -0.5),
        'pool_scale': gain(ks[12], (L, POOL_WIDTH)),
        'gla_norm_g': gain(ks[13], (L, GLA_HEADS, GLA_DV)),
        'w_out': nrm(ks[14], (L, MIX_WIDTH, D_MODEL), MIX_WIDTH ** -0.5),
        'mem_norm_g': gain(ks[15], (L, D_MODEL)),
        'w_km': nrm(ks[16], (L, D_MODEL, D_MODEL), D_MODEL ** -0.5),
        'w_vm': nrm(ks[17], (L, D_MODEL, D_MODEL), D_MODEL ** -0.5),
        'norm_mem_g': gain(ks[18], (L, D_MODEL)),
        'w_qm': nrm(ks[19], (L, D_MODEL, D_MODEL), D_MODEL ** -0.5),
        'w_om': nrm(ks[20], (L, D_MODEL, D_MODEL), D_MODEL ** -0.5),
        'norm_ffn_g': gain(ks[21], (L, D_MODEL)),
        'w_up': nrm(ks[22], (L, D_MODEL, D_FF), D_MODEL ** -0.5),
        'w_down': nrm(ks[23], (L, D_FF, D_MODEL), D_FF ** -0.5),
        'norm_final_g': gain(ks[24], (D_MODEL,)),
    }


def reference(x_prompt, x_sample, state_pool, state_gla, cache_mem_k, cache_mem_v, mem_prompt,
              norm_mix_g, w_in, w_forget_up, b_forget, w_pool, pool_scale, gla_norm_g, w_out,
              mem_norm_g, w_km, w_vm, norm_mem_g, w_qm, w_om, norm_ffn_g, w_up, w_down, norm_final_g):
    b_p = x_prompt.shape[0]
    xp = x_prompt
    xs = x_sample
    pool_p, gla_p, mk_p, mv_p, pool_s, gla_s = [], [], [], [], [], []
    for l in range(DEPTH):
        lw = (norm_mix_g[l], w_in[l], w_forget_up[l], b_forget[l], w_pool[l], pool_scale[l], gla_norm_g[l],
              w_out[l], norm_mem_g[l], w_qm[l], w_om[l], norm_ffn_g[l], w_up[l], w_down[l])
        mk, mv = _mem_kv(mem_prompt, mem_norm_g[l], w_km[l], w_vm[l])
        xp, sp, sg = _layer(xp, jnp.zeros((b_p, 0, POOL_WIDTH), xp.dtype),
                            jnp.zeros((b_p, GLA_HEADS, GLA_DK, GLA_DV), xp.dtype), mk, mv, *lw)
        pool_p.append(sp)
        gla_p.append(sg)
        mk_p.append(mk)
        mv_p.append(mv)
        xs, ss, gs = _layer(xs, state_pool[l], state_gla[l], cache_mem_k[l], cache_mem_v[l], *lw)
        pool_s.append(ss)
        gla_s.append(gs)
    y_prompt = _rmsnorm(xp, norm_final_g)
    y_sample = _rmsnorm(xs, norm_final_g)
    return (y_prompt, y_sample, jnp.stack(pool_p), jnp.stack(gla_p), jnp.stack(mk_p), jnp.stack(mv_p),
            jnp.stack(pool_s), jnp.stack(gla_s))
```

```python
import functools

import jax
import jax.numpy as jnp
from jax import lax
from jax.experimental import pallas as pl
from jax.experimental.pallas import tpu as pltpu

F32 = jnp.float32
BF16 = jnp.bfloat16

D_MODEL = 1024
BATCH = 8
SEQ = 2048
DEC_BATCH = 128
POOL_WIDTH = 512
POOL_WINDOWS = (2, 4, 8, 16)
POOL_GROUP_WIDTH = 128
POOL_BUF = 15
POOL_TAIL = 16
GLA_WIDTH = 512
GLA_HEADS = 4
GLA_DV = 128
GLA_DK = 64
GLA_KEY_WIDTH = 256
GLA_GATE_RANK = 16
GLA_GATE_NORM = 16.0
GLA_CHUNK = 64
MEM_TOKENS = 256
MEM_HEADS = 4
MEM_HEAD_DIM = 256
D_FF = 4096
EPS = 1e-6

LANES = 128
IN_COLS_PAD = 2048 + LANES
C_U, C_Q, C_K, C_V, C_G, C_F = 0, 512, 768, 1024, 1536, 2048

VMEM_LIMIT = 56 * 1024 * 1024

TM_MIX = 256
TM_TAIL = 512
FF_CHUNK = 1024
SB_MIX = 8
SB_ATT = 4


def _rms(x, g):
    return x * lax.rsqrt(jnp.mean(x * x, axis=-1, keepdims=True) + EPS) * g


def _dot(a, b):
    return jnp.dot(a, b, preferred_element_type=F32)


def _dot_nt(a, b):
    return lax.dot_general(a, b, (((1,), (1,)), ((), ())), preferred_element_type=F32)


def _log_sigmoid(x):
    return jnp.minimum(x, 0.0) - jnp.log1p(jnp.exp(-jnp.abs(x)))


def _silu(x):
    return x / (1.0 + jnp.exp(-x))


def _const_spec(shape):
    zeros = (0,) * len(shape)
    return pl.BlockSpec(shape, lambda *_: zeros, pipeline_mode=pl.Buffered(1))


def _in_proj(x, g_mix, w_in_ref):
    h = _rms(x, g_mix).astype(BF16)
    return _dot(h, w_in_ref[...])


def _log_forget(f_low, w_fu_ref, b_f):
    pre = _dot(f_low.astype(BF16), w_fu_ref[...]) + b_f
    return _log_sigmoid(pre) * (1.0 / GLA_GATE_NORM)


def _pool_project(pooled, w_pool_ref, pool_scale):
    outs = []
    for gi in range(len(POOL_WINDOWS)):
        c0 = gi * POOL_GROUP_WIDTH
        outs.append(_dot(pooled[gi].astype(BF16), w_pool_ref[gi]) * pool_scale[:, c0:c0 + POOL_GROUP_WIDTH])
    return jnp.concatenate(outs, axis=-1)


def _head_norm_gate(o, gate, g_gla):
    outs = []
    for hh in range(GLA_HEADS):
        c0 = hh * GLA_DV
        outs.append(_rms(o[:, c0:c0 + GLA_DV], g_gla[:, c0:c0 + GLA_DV]))
    return jnp.concatenate(outs, axis=-1) * _silu(gate)


def _memkv_body(mem_ref, g_ref, w_ref, mk_ref, mv_ref, mk16_ref, mv16_ref):
    mn = _rms(mem_ref[...], g_ref[...]).astype(BF16)
    kv = _dot(mn, w_ref[...])
    k = kv[:, :D_MODEL]
    v = kv[:, D_MODEL:]
    mk_ref[...] = k
    mv_ref[...] = v
    mk16_ref[...] = k.astype(BF16)
    mv16_ref[...] = v.astype(BF16)


def _mem_kv(mem2d, g_mem, w_kv16):
    n = mem2d.shape[0]
    blk = pl.BlockSpec((MEM_TOKENS, D_MODEL), lambda b: (b, 0))
    return pl.pallas_call(
        _memkv_body,
        grid=(n // MEM_TOKENS,),
        in_specs=[blk, _const_spec((1, D_MODEL)), _const_spec((D_MODEL, 2 * D_MODEL))],
        out_specs=[blk, blk, blk, blk],
        out_shape=[jax.ShapeDtypeStruct((n, D_MODEL), F32), jax.ShapeDtypeStruct((n, D_MODEL), F32),
                   jax.ShapeDtypeStruct((n, D_MODEL), BF16), jax.ShapeDtypeStruct((n, D_MODEL), BF16)],
        compiler_params=pltpu.CompilerParams(dimension_semantics=("arbitrary",), vmem_limit_bytes=VMEM_LIMIT),
        name="mem_kv",
    )(mem2d, g_mem, w_kv16)


def _mix_prompt_body(x_ref, g_mix_ref, w_in_ref, w_fu_ref, b_f_ref, w_pool_ref, pool_scale_ref, g_gla_ref,
                     w_out_ref, x1_ref, tail_ref, state_ref, ubuf, st_ref, mixbuf, obuf):
    t = pl.program_id(1)
    tm = TM_MIX

    @pl.when(t == 0)
    def _():
        ubuf[0:POOL_TAIL, :] = jnp.zeros((POOL_TAIL, POOL_WIDTH), F32)
        st_ref[...] = jnp.zeros(st_ref.shape, F32)

    x = x_ref[...]
    proj = _in_proj(x, g_mix_ref[...], w_in_ref)
    u = proj[:, C_U:C_U + POOL_WIDTH]

    ubuf[POOL_TAIL:POOL_TAIL + tm, :] = u
    pos1 = t * tm + 1 + lax.broadcasted_iota(jnp.int32, (tm, POOL_GROUP_WIDTH), 0)
    pooled = []
    for gi, w in enumerate(POOL_WINDOWS):
        c0 = gi * POOL_GROUP_WIDTH
        win = u[:, c0:c0 + POOL_GROUP_WIDTH]
        for j in range(1, w):
            win = win + ubuf[POOL_TAIL - j:POOL_TAIL - j + tm, c0:c0 + POOL_GROUP_WIDTH]
        count = jnp.minimum(pos1, w).astype(F32)
        pooled.append(win / count - u[:, c0:c0 + POOL_GROUP_WIDTH])
    mixbuf[:, 0:POOL_WIDTH] = _pool_project(pooled, w_pool_ref, pool_scale_ref[...]).astype(BF16)
    tail = ubuf[tm:tm + POOL_TAIL, :]
    ubuf[0:POOL_TAIL, :] = tail
    tail_ref[...] = tail

    log_f = _log_forget(proj[:, C_F:C_F + LANES], w_fu_ref, b_f_ref[...])
    q = proj[:, C_Q:C_Q + GLA_KEY_WIDTH] * (GLA_DK ** -0.5)
    k = proj[:, C_K:C_K + GLA_KEY_WIDTH]
    v = proj[:, C_V:C_V + GLA_WIDTH]
    v16 = v.astype(BF16)
    vt16 = v.T.astype(BF16)
    row = lax.broadcasted_iota(jnp.int32, (GLA_CHUNK, GLA_CHUNK), 0)
    col = lax.broadcasted_iota(jnp.int32, (GLA_CHUNK, GLA_CHUNK), 1)
    causal = row >= col
    tril = causal.astype(F32)
    for c in range(tm // GLA_CHUNK):
        r0 = c * GLA_CHUNK
        cum = jnp.dot(tril, log_f[r0:r0 + GLA_CHUNK, :], precision=lax.Precision.HIGHEST,
                      preferred_element_type=F32)
        cum_last = cum[GLA_CHUNK - 1:GLA_CHUNK, :]
        q_dec = q[r0:r0 + GLA_CHUNK, :] * jnp.exp(cum)
        k_c = k[r0:r0 + GLA_CHUNK, :]
        k_inv = k_c * jnp.exp(-cum)
        k_end = k_c * jnp.exp(cum_last - cum)
        decay = jnp.exp(cum_last)
        st = st_ref[...]
        new_st = []
        for hh in range(GLA_HEADS):
            k0 = hh * GLA_DK
            v0 = hh * GLA_DV
            q_h = q_dec[:, k0:k0 + GLA_DK].astype(BF16)
            scores = _dot_nt(q_h, k_inv[:, k0:k0 + GLA_DK].astype(BF16))
            scores = jnp.where(causal, scores, 0.0).astype(BF16)
            o_h = _dot(scores, v16[r0:r0 + GLA_CHUNK, v0:v0 + GLA_DV])
            o_h = o_h + _dot_nt(q_h, st[:, k0:k0 + GLA_DK].astype(BF16))
            obuf[r0:r0 + GLA_CHUNK, v0:v0 + GLA_DV] = o_h
            upd = _dot(vt16[v0:v0 + GLA_DV, r0:r0 + GLA_CHUNK], k_end[:, k0:k0 + GLA_DK].astype(BF16))
            new_st.append(st[:, k0:k0 + GLA_DK] * decay[:, k0:k0 + GLA_DK] + upd)
        st_ref[...] = jnp.concatenate(new_st, axis=-1)

    @pl.when(t == pl.num_programs(1) - 1)
    def _():
        state_ref[...] = st_ref[...].T

    gate = proj[:, C_G:C_G + GLA_WIDTH]
    mixbuf[:, POOL_WIDTH:] = _head_norm_gate(obuf[...], gate, g_gla_ref[...]).astype(BF16)
    x1_ref[...] = x + _dot(mixbuf[...], w_out_ref[...])


def _mix_prompt(x2d, g_mix, w_in16, w_fu16, b_f, w_pool16, pool_scale, g_gla, w_out16):
    nt = SEQ // TM_MIX
    return pl.pallas_call(
        _mix_prompt_body,
        grid=(BATCH, nt),
        in_specs=[pl.BlockSpec((TM_MIX, D_MODEL), lambda b, t: (b * nt + t, 0)),
                  _const_spec((1, D_MODEL)), _const_spec((D_MODEL, IN_COLS_PAD)),
                  _const_spec((LANES, GLA_KEY_WIDTH)), _const_spec((1, GLA_KEY_WIDTH)),
                  _const_spec((len(POOL_WINDOWS), POOL_GROUP_WIDTH, POOL_GROUP_WIDTH)),
                  _const_spec((1, POOL_WIDTH)), _const_spec((1, GLA_WIDTH)),
                  _const_spec((D_MODEL, D_MODEL))],
        out_specs=[pl.BlockSpec((TM_MIX, D_MODEL), lambda b, t: (b * nt + t, 0)),
                   pl.BlockSpec((None, POOL_TAIL, POOL_WIDTH), lambda b, t: (b, 0, 0)),
                   pl.BlockSpec((None, GLA_KEY_WIDTH, GLA_DV), lambda b, t: (b, 0, 0))],
        out_shape=[jax.ShapeDtypeStruct((BATCH * SEQ, D_MODEL), F32),
                   jax.ShapeDtypeStruct((BATCH, POOL_TAIL, POOL_WIDTH), F32),
                   jax.ShapeDtypeStruct((BATCH, GLA_KEY_WIDTH, GLA_DV), F32)],
        scratch_shapes=[pltpu.VMEM((POOL_TAIL + TM_MIX, POOL_WIDTH), F32),
                        pltpu.VMEM((GLA_DV, GLA_KEY_WIDTH), F32),
                        pltpu.VMEM((TM_MIX, D_MODEL), BF16),
                        pltpu.VMEM((TM_MIX, GLA_WIDTH), F32)],
        compiler_params=pltpu.CompilerParams(dimension_semantics=("arbitrary", "arbitrary"),
                                             vmem_limit_bytes=VMEM_LIMIT),
        name="mix_prompt",
    )(x2d, g_mix, w_in16, w_fu16, b_f, w_pool16, pool_scale, g_gla, w_out16)


def _mem_query(x1, g_mem, w_qm_ref):
    return _dot(_rms(x1, g_mem).astype(BF16), w_qm_ref[...])


def _ffn_tail(x1, ctx16, w_om_ref, g_ffn, w_up_ref, w_down_ref, g_final):
    x2 = x1 + _dot(ctx16, w_om_ref[...])
    hf = _rms(x2, g_ffn).astype(BF16)
    acc = x2
    for f0 in range(0, D_FF, FF_CHUNK):
        a = jnp.maximum(_dot(hf, w_up_ref[:, f0:f0 + FF_CHUNK]), 0.0)
        acc = acc + _dot((a * a).astype(BF16), w_down_ref[f0:f0 + FF_CHUNK, :])
    return _rms(acc, g_final)


def _tail_prompt_body(x1_ref, mk_ref, mv_ref, g_mem_ref, w_qm_ref, w_om_ref, g_ffn_ref, w_up_ref, w_down_ref,
                      g_final_ref, y_ref, ctxbuf):
    x1 = x1_ref[...]
    qm = _mem_query(x1, g_mem_ref[...], w_qm_ref)
    for hh in range(MEM_HEADS):
        c0 = hh * MEM_HEAD_DIM
        s = _dot_nt(qm[:, c0:c0 + MEM_HEAD_DIM].astype(BF16), mk_ref[:, c0:c0 + MEM_HEAD_DIM])
        s = s * (MEM_HEAD_DIM ** -0.5)
        e = jnp.exp(s - jnp.max(s, axis=-1, keepdims=True))
        p = e / jnp.sum(e, axis=-1, keepdims=True)
        ctxbuf[:, c0:c0 + MEM_HEAD_DIM] = _dot(p.astype(BF16), mv_ref[:, c0:c0 + MEM_HEAD_DIM]).astype(BF16)
    y_ref[...] = _ffn_tail(x1, ctxbuf[...], w_om_ref, g_ffn_ref[...], w_up_ref, w_down_ref, g_final_ref[...])


def _tail_prompt(x1, mk16, mv16, g_mem, w_qm16, w_om16, g_ffn, w_up16, w_down16, g_final):
    n = x1.shape[0]
    tiles_per_seq = SEQ // TM_TAIL
    tok = pl.BlockSpec((TM_TAIL, D_MODEL), lambda i: (i, 0))
    mem = pl.BlockSpec((MEM_TOKENS, D_MODEL), lambda i: (i // tiles_per_seq, 0))
    return pl.pallas_call(
        _tail_prompt_body,
        grid=(n // TM_TAIL,),
        in_specs=[tok, mem, mem, _const_spec((1, D_MODEL)), _const_spec((D_MODEL, D_MODEL)),
                  _const_spec((D_MODEL, D_MODEL)), _const_spec((1, D_MODEL)), _const_spec((D_MODEL, D_FF)),
                  _const_spec((D_FF, D_MODEL)), _const_spec((1, D_MODEL))],
        out_specs=tok,
        out_shape=jax.ShapeDtypeStruct((n, D_MODEL), F32),
        scratch_shapes=[pltpu.VMEM((TM_TAIL, D_MODEL), BF16)],
        compiler_params=pltpu.CompilerParams(dimension_semantics=("arbitrary",), vmem_limit_bytes=VMEM_LIMIT),
        name="tail_prompt",
    )(x1, mk16, mv16, g_mem, w_qm16, w_om16, g_ffn, w_up16, w_down16, g_final)


def _mix_sample_body(x_ref, pool_ref, s0_ref, g_mix_ref, w_in_ref, w_fu_ref, b_f_ref, w_pool_ref, pool_scale_ref,
                     g_gla_ref, w_out_ref, x1_ref, pool_out_ref, s_out_ref,
                     qt_ref, kt_ref, at_ref, v_ref, o_ref, gate_ref, mixbuf):
    i = pl.program_id(0)
    nb = DEC_BATCH

    @pl.when(i == 0)
    def _():
        proj = _in_proj(x_ref[...], g_mix_ref[...], w_in_ref)
        u = proj[:, C_U:C_U + POOL_WIDTH]
        pooled = []
        for gi, w in enumerate(POOL_WINDOWS):
            c0 = gi * POOL_GROUP_WIDTH
            win = u[:, c0:c0 + POOL_GROUP_WIDTH]
            for j in range(1, w):
                l0 = (POOL_BUF - j) * POOL_WIDTH + c0
                win = win + pool_ref[:, l0:l0 + POOL_GROUP_WIDTH]
            pooled.append(win / float(w) - u[:, c0:c0 + POOL_GROUP_WIDTH])
        mixbuf[:, 0:POOL_WIDTH] = _pool_project(pooled, w_pool_ref, pool_scale_ref[...]).astype(BF16)
        pool_out_ref[:, 0:(POOL_BUF - 1) * POOL_WIDTH] = pool_ref[:, POOL_WIDTH:]
        pool_out_ref[:, (POOL_BUF - 1) * POOL_WIDTH:] = u

        log_f = _log_forget(proj[:, C_F:C_F + LANES], w_fu_ref, b_f_ref[...])
        alpha = jnp.exp(log_f)
        q_dec = proj[:, C_Q:C_Q + GLA_KEY_WIDTH] * (GLA_DK ** -0.5) * alpha
        k = proj[:, C_K:C_K + GLA_KEY_WIDTH]
        k_inv = k * jnp.exp(-log_f)
        v = proj[:, C_V:C_V + GLA_WIDTH]
        qk = q_dec * k_inv
        o_intra = []
        for hh in range(GLA_HEADS):
            s_h = jnp.sum(qk[:, hh * GLA_DK:(hh + 1) * GLA_DK], axis=-1, keepdims=True)
            o_intra.append(s_h * v[:, hh * GLA_DV:(hh + 1) * GLA_DV])
        o_ref[...] = jnp.concatenate(o_intra, axis=-1)
        v_ref[...] = v
        gate_ref[...] = proj[:, C_G:C_G + GLA_WIDTH]
        qt_ref[...] = q_dec.T
        kt_ref[...] = k.T
        at_ref[...] = alpha.T

    shift = (nb - i * SB_MIX) % nb
    qt = pltpu.roll(qt_ref[...], shift, axis=1)
    kt = pltpu.roll(kt_ref[...], shift, axis=1)
    at = pltpu.roll(at_ref[...], shift, axis=1)
    b0 = pl.multiple_of(i * SB_MIX, SB_MIX)
    v_blk = v_ref[pl.ds(b0, SB_MIX), :]
    o_rows = []
    for j in range(SB_MIX):
        o_heads = []
        for hh in range(GLA_HEADS):
            k0 = hh * GLA_DK
            v0 = hh * GLA_DV
            s_old = s0_ref[j, k0:k0 + GLA_DK, :]
            q_col = jnp.broadcast_to(qt[k0:k0 + GLA_DK, j:j + 1], (GLA_DK, GLA_DV))
            k_col = jnp.broadcast_to(kt[k0:k0 + GLA_DK, j:j + 1], (GLA_DK, GLA_DV))
            a_col = jnp.broadcast_to(at[k0:k0 + GLA_DK, j:j + 1], (GLA_DK, GLA_DV))
            v_row = v_blk[j:j + 1, v0:v0 + GLA_DV]
            s_out_ref[j, k0:k0 + GLA_DK, :] = a_col * s_old + k_col * v_row
            o_heads.append(jnp.sum(q_col * s_old, axis=0, keepdims=True))
        o_rows.append(jnp.concatenate(o_heads, axis=-1))
    o_ref[pl.ds(b0, SB_MIX), :] = o_ref[pl.ds(b0, SB_MIX), :] + jnp.concatenate(o_rows, axis=0)

    @pl.when(i == pl.num_programs(0) - 1)
    def _():
        mixbuf[:, POOL_WIDTH:] = _head_norm_gate(o_ref[...], gate_ref[...], g_gla_ref[...]).astype(BF16)
        x1_ref[...] = x_ref[...] + _dot(mixbuf[...], w_out_ref[...])


def _mix_sample(xs, pool2d, s0, g_mix, w_in16, w_fu16, b_f, w_pool16, pool_scale, g_gla, w_out16):
    nb = DEC_BATCH
    state_blk = pl.BlockSpec((SB_MIX, GLA_KEY_WIDTH, GLA_DV), lambda i: (i, 0, 0))
    return pl.pallas_call(
        _mix_sample_body,
        grid=(nb // SB_MIX,),
        in_specs=[_const_spec((nb, D_MODEL)), _const_spec((nb, POOL_BUF * POOL_WIDTH)), state_blk,
                  _const_spec((1, D_MODEL)), _const_spec((D_MODEL, IN_COLS_PAD)),
                  _const_spec((LANES, GLA_KEY_WIDTH)), _const_spec((1, GLA_KEY_WIDTH)),
                  _const_spec((len(POOL_WINDOWS), POOL_GROUP_WIDTH, POOL_GROUP_WIDTH)),
                  _const_spec((1, POOL_WIDTH)), _const_spec((1, GLA_WIDTH)),
                  _const_spec((D_MODEL, D_MODEL))],
        out_specs=[pl.BlockSpec((nb, D_MODEL), lambda i: (0, 0)),
                   pl.BlockSpec((nb, POOL_BUF * POOL_WIDTH), lambda i: (0, 0)),
                   state_blk],
        out_shape=[jax.ShapeDtypeStruct((nb, D_MODEL), F32),
                   jax.ShapeDtypeStruct((nb, POOL_BUF * POOL_WIDTH), F32),
                   jax.ShapeDtypeStruct((nb, GLA_KEY_WIDTH, GLA_DV), F32)],
        scratch_shapes=[pltpu.VMEM((GLA_KEY_WIDTH, nb), F32), pltpu.VMEM((GLA_KEY_WIDTH, nb), F32),
                        pltpu.VMEM((GLA_KEY_WIDTH, nb), F32), pltpu.VMEM((nb, GLA_WIDTH), F32),
                        pltpu.VMEM((nb, GLA_WIDTH), F32), pltpu.VMEM((nb, GLA_WIDTH), F32),
                        pltpu.VMEM((nb, D_MODEL), BF16)],
        compiler_params=pltpu.CompilerParams(dimension_semantics=("arbitrary",), vmem_limit_bytes=VMEM_LIMIT),
        name="mix_sample",
    )(xs, pool2d, s0, g_mix, w_in16, w_fu16, b_f, w_pool16, pool_scale, g_gla, w_out16)


def _tail_sample_body(x1_ref, ck_ref, cv_ref, g_mem_ref, w_qm_ref, w_om_ref, g_ffn_ref, w_up_ref, w_down_ref,
                      g_final_ref, y_ref, qm_ref, ctx_ref):
    i = pl.program_id(0)

    @pl.when(i == 0)
    def _():
        qm_ref[...] = _mem_query(x1_ref[...], g_mem_ref[...], w_qm_ref) * (MEM_HEAD_DIM ** -0.5)

    for j in range(SB_ATT):
        q_row = qm_ref[pl.ds(i * SB_ATT + j, 1), :]
        ctx_heads = []
        for hh in range(MEM_HEADS):
            c0 = hh * MEM_HEAD_DIM
            s = jnp.sum(ck_ref[j, :, c0:c0 + MEM_HEAD_DIM] * q_row[:, c0:c0 + MEM_HEAD_DIM],
                        axis=-1, keepdims=True)
            e = jnp.exp(s - jnp.max(s, axis=0, keepdims=True))
            p = e / jnp.sum(e, axis=0, keepdims=True)
            ctx_heads.append(jnp.sum(p * cv_ref[j, :, c0:c0 + MEM_HEAD_DIM], axis=0, keepdims=True))
        ctx_ref[pl.ds(i * SB_ATT + j, 1), :] = jnp.concatenate(ctx_heads, axis=-1)

    @pl.when(i == pl.num_programs(0) - 1)
    def _():
        y_ref[...] = _ffn_tail(x1_ref[...], ctx_ref[...].astype(BF16), w_om_ref, g_ffn_ref[...], w_up_ref,
                               w_down_ref, g_final_ref[...])


def _tail_sample(x1, ck, cv, g_mem, w_qm16, w_om16, g_ffn, w_up16, w_down16, g_final):
    nb = DEC_BATCH
    cache = pl.BlockSpec((SB_ATT, MEM_TOKENS, D_MODEL), lambda i: (i, 0, 0))
    return pl.pallas_call(
        _tail_sample_body,
        grid=(nb // SB_ATT,),
        in_specs=[_const_spec((nb, D_MODEL)), cache, cache, _const_spec((1, D_MODEL)),
                  _const_spec((D_MODEL, D_MODEL)), _const_spec((D_MODEL, D_MODEL)), _const_spec((1, D_MODEL)),
                  _const_spec((D_MODEL, D_FF)), _const_spec((D_FF, D_MODEL)), _const_spec((1, D_MODEL))],
        out_specs=pl.BlockSpec((nb, D_MODEL), lambda i: (0, 0)),
        out_shape=jax.ShapeDtypeStruct((nb, D_MODEL), F32),
        scratch_shapes=[pltpu.VMEM((nb, D_MODEL), F32), pltpu.VMEM((nb, D_MODEL), F32)],
        compiler_params=pltpu.CompilerParams(dimension_semantics=("arbitrary",), vmem_limit_bytes=VMEM_LIMIT),
        name="tail_sample",
    )(x1, ck, cv, g_mem, w_qm16, w_om16, g_ffn, w_up16, w_down16, g_final)


def kernel(x_prompt, x_sample, state_pool, state_gla, cache_mem_k, cache_mem_v, mem_prompt, norm_mix_g, w_in,
           w_forget_up, b_forget, w_pool, pool_scale, gla_norm_g, w_out, mem_norm_g, w_km, w_vm, norm_mem_g, w_qm,
           w_om, norm_ffn_g, w_up, w_down, norm_final_g):
    assert x_prompt.shape == (BATCH, SEQ, D_MODEL) and x_sample.shape == (DEC_BATCH, 1, D_MODEL)
    assert w_in.shape[0] == 1, "single layer"
    pad_cols = IN_COLS_PAD - w_in.shape[-1]
    w_in16 = jnp.pad(w_in[0], ((0, 0), (0, pad_cols))).astype(BF16)
    w_fu16 = jnp.pad(w_forget_up[0], ((0, LANES - GLA_GATE_RANK), (0, 0))).astype(BF16)
    w_pool16 = w_pool[0].astype(BF16)
    w_out16 = w_out[0].astype(BF16)
    w_kv16 = jnp.concatenate([w_km[0], w_vm[0]], axis=1).astype(BF16)
    w_qm16 = w_qm[0].astype(BF16)
    w_om16 = w_om[0].astype(BF16)
    w_up16 = w_up[0].astype(BF16)
    w_down16 = w_down[0].astype(BF16)
    g_mix = norm_mix_g.reshape(1, D_MODEL)
    g_memn = mem_norm_g.reshape(1, D_MODEL)
    g_mem = norm_mem_g.reshape(1, D_MODEL)
    g_ffn = norm_ffn_g.reshape(1, D_MODEL)
    g_final = norm_final_g.reshape(1, D_MODEL)
    g_gla = gla_norm_g.reshape(1, GLA_WIDTH)
    b_f = b_forget.reshape(1, GLA_KEY_WIDTH)
    p_scale = pool_scale.reshape(1, POOL_WIDTH)

    mk, mv, mk16, mv16 = _mem_kv(mem_prompt.reshape(BATCH * MEM_TOKENS, D_MODEL), g_memn, w_kv16)
    x1p, tail_p, gla_p = _mix_prompt(x_prompt.reshape(BATCH * SEQ, D_MODEL), g_mix, w_in16, w_fu16, b_f, w_pool16,
                                     p_scale, g_gla, w_out16)
    y_p = _tail_prompt(x1p, mk16, mv16, g_mem, w_qm16, w_om16, g_ffn, w_up16, w_down16, g_final)

    x1s, pool_s, gla_s = _mix_sample(x_sample.reshape(DEC_BATCH, D_MODEL),
                                     state_pool.reshape(DEC_BATCH, POOL_BUF * POOL_WIDTH),
                                     state_gla.reshape(DEC_BATCH, GLA_KEY_WIDTH, GLA_DV), g_mix, w_in16, w_fu16,
                                     b_f, w_pool16, p_scale, g_gla, w_out16)
    y_s = _tail_sample(x1s, cache_mem_k.reshape(DEC_BATCH, MEM_TOKENS, D_MODEL),
                       cache_mem_v.reshape(DEC_BATCH, MEM_TOKENS, D_MODEL), g_mem, w_qm16, w_om16, g_ffn, w_up16,
                       w_down16, g_final)

    kv_shape = (1, BATCH, MEM_TOKENS, MEM_HEADS, MEM_HEAD_DIM)
    return (y_p.reshape(BATCH, SEQ, D_MODEL),
            y_s.reshape(DEC_BATCH, 1, D_MODEL),
            tail_p[:, POOL_TAIL - POOL_BUF:, :].reshape(1, BATCH, POOL_BUF, POOL_WIDTH),
            gla_p.reshape(1, BATCH, GLA_HEADS, GLA_DK, GLA_DV),
            mk.reshape(kv_shape),
            mv.reshape(kv_shape),
            pool_s.reshape(1, DEC_BATCH, POOL_BUF, POOL_WIDTH),
            gla_s.reshape(1, DEC_BATCH, GLA_HEADS, GLA_DK, GLA_DV))
```

```python
import functools

import jax
import jax.numpy as jnp
from jax import lax
from jax.experimental import pallas as pl
from jax.experimental.pallas import tpu as pltpu

F32 = jnp.float32
BF16 = jnp.bfloat16

D_MODEL = 1024
BATCH = 8
SEQ = 2048
DEC_BATCH = 128
POOL_WIDTH = 512
POOL_WINDOWS = (2, 4, 8, 16)
POOL_GROUP_WIDTH = 128
POOL_BUF = 15
POOL_TAIL = 16
GLA_WIDTH = 512
GLA_HEADS = 4
GLA_DV = 128
GLA_DK = 64
GLA_KEY_WIDTH = 256
GLA_GATE_RANK = 16
GLA_GATE_NORM = 16.0
GLA_CHUNK = 64
MEM_TOKENS = 256
MEM_HEADS = 4
MEM_HEAD_DIM = 256
D_FF = 4096
EPS = 1e-6

LANES = 128
IN_COLS_PAD = 2048 + LANES
C_U, C_Q, C_K, C_V, C_G, C_F = 0, 512, 768, 1024, 1536, 2048

VMEM_LIMIT = 56 * 1024 * 1024

TM_MIX = 256
TM_TAIL = 512
FF_CHUNK = 1024
SB_MIX = 8
SB_ATT = 4


def _rms(x, g):
    return x * lax.rsqrt(jnp.mean(x * x, axis=-1, keepdims=True) + EPS) * g


def _dot(a, b):
    return jnp.dot(a, b, preferred_element_type=F32)


def _dot_nt(a, b):
    return lax.dot_general(a, b, (((1,), (1,)), ((), ())), preferred_element_type=F32)


def _log_sigmoid(x):
    return jnp.minimum(x, 0.0) - jnp.log1p(jnp.exp(-jnp.abs(x)))


def _silu(x):
    return x / (1.0 + jnp.exp(-x))


def _const_spec(shape):
    zeros = (0,) * len(shape)
    return pl.BlockSpec(shape, lambda *_: zeros, pipeline_mode=pl.Buffered(1))


def _in_proj(x, g_mix, w_in_ref):
    h = _rms(x, g_mix).astype(BF16)
    return _dot(h, w_in_ref[...])


def _log_forget(f_low, w_fu_ref, b_f):
    pre = _dot(f_low.astype(BF16), w_fu_ref[...]) + b_f
    return _log_sigmoid(pre) * (1.0 / GLA_GATE_NORM)


def _pool_project(pooled, w_pool_ref, pool_scale):
    outs = []
    for gi in range(len(POOL_WINDOWS)):
        c0 = gi * POOL_GROUP_WIDTH
        outs.append(_dot(pooled[gi].astype(BF16), w_pool_ref[gi]) * pool_scale[:, c0:c0 + POOL_GROUP_WIDTH])
    return jnp.concatenate(outs, axis=-1)


def _head_norm_gate(o, gate, g_gla):
    outs = []
    for hh in range(GLA_HEADS):
        c0 = hh * GLA_DV
        outs.append(_rms(o[:, c0:c0 + GLA_DV], g_gla[:, c0:c0 + GLA_DV]))
    return jnp.concatenate(outs, axis=-1) * _silu(gate)


def _memkv_body(mem_ref, g_ref, w_ref, mk_ref, mv_ref, mk16_ref, mv16_ref):
    mn = _rms(mem_ref[...], g_ref[...]).astype(BF16)
    kv = _dot(mn, w_ref[...])
    k = kv[:, :D_MODEL]
    v = kv[:, D_MODEL:]
    mk_ref[...] = k
    mv_ref[...] = v
    mk16_ref[...] = k.astype(BF16)
    mv16_ref[...] = v.astype(BF16)


def _mem_kv(mem2d, g_mem, w_kv16):
    n = mem2d.shape[0]
    blk = pl.BlockSpec((MEM_TOKENS, D_MODEL), lambda b: (b, 0))
    return pl.pallas_call(
        _memkv_body,
        grid=(n // MEM_TOKENS,),
        in_specs=[blk, _const_spec((1, D_MODEL)), _const_spec((D_MODEL, 2 * D_MODEL))],
        out_specs=[blk, blk, blk, blk],
        out_shape=[jax.ShapeDtypeStruct((n, D_MODEL), F32), jax.ShapeDtypeStruct((n, D_MODEL), F32),
                   jax.ShapeDtypeStruct((n, D_MODEL), BF16), jax.ShapeDtypeStruct((n, D_MODEL), BF16)],
        compiler_params=pltpu.CompilerParams(dimension_semantics=("arbitrary",), vmem_limit_bytes=VMEM_LIMIT),
        name="mem_kv",
    )(mem2d, g_mem, w_kv16)


def _mix_prompt_body(x_ref, g_mix_ref, w_in_ref, w_fu_ref, b_f_ref, w_pool_ref, pool_scale_ref, g_gla_ref,
                     w_out_ref, x1_ref, tail_ref, state_ref, ubuf, st_ref, mixbuf, obuf):
    t = pl.program_id(1)
    tm = TM_MIX

    @pl.when(t == 0)
    def _():
        ubuf[0:POOL_TAIL, :] = jnp.zeros((POOL_TAIL, POOL_WIDTH), F32)
        st_ref[...] = jnp.zeros(st_ref.shape, F32)

    x = x_ref[...]
    proj = _in_proj(x, g_mix_ref[...], w_in_ref)
    u = proj[:, C_U:C_U + POOL_WIDTH]

    ubuf[POOL_TAIL:POOL_TAIL + tm, :] = u
    pos1 = t * tm + 1 + lax.broadcasted_iota(jnp.int32, (tm, POOL_GROUP_WIDTH), 0)
    pooled = []
    for gi, w in enumerate(POOL_WINDOWS):
        c0 = gi * POOL_GROUP_WIDTH
        win = u[:, c0:c0 + POOL_GROUP_WIDTH]
        for j in range(1, w):
            win = win + ubuf[POOL_TAIL - j:POOL_TAIL - j + tm, c0:c0 + POOL_GROUP_WIDTH]
        count = jnp.minimum(pos1, w).astype(F32)
        pooled.append(win / count - u[:, c0:c0 + POOL_GROUP_WIDTH])
    mixbuf[:, 0:POOL_WIDTH] = _pool_project(pooled, w_pool_ref, pool_scale_ref[...]).astype(BF16)
    tail = ubuf[tm:tm + POOL_TAIL, :]
    ubuf[0:POOL_TAIL, :] = tail
    tail_ref[...] = tail

    log_f = _log_forget(proj[:, C_F:C_F + LANES], w_fu_ref, b_f_ref[...])
    q = proj[:, C_Q:C_Q + GLA_KEY_WIDTH] * (GLA_DK ** -0.5)
    k = proj[:, C_K:C_K + GLA_KEY_WIDTH]
    v = proj[:, C_V:C_V + GLA_WIDTH]
    v16 = v.astype(BF16)
    vt16 = v.T.astype(BF16)
    row = lax.broadcasted_iota(jnp.int32, (GLA_CHUNK, GLA_CHUNK), 0)
    col = lax.broadcasted_iota(jnp.int32, (GLA_CHUNK, GLA_CHUNK), 1)
    causal = row >= col
    tril = causal.astype(F32)
    for c in range(tm // GLA_CHUNK):
        r0 = c * GLA_CHUNK
        cum = jnp.dot(tril, log_f[r0:r0 + GLA_CHUNK, :], precision=lax.Precision.HIGHEST,
                      preferred_element_type=F32)
        cum_last = cum[GLA_CHUNK - 1:GLA_CHUNK, :]
        q_dec = q[r0:r0 + GLA_CHUNK, :] * jnp.exp(cum)
        k_c = k[r0:r0 + GLA_CHUNK, :]
        k_inv = k_c * jnp.exp(-cum)
        k_end = k_c * jnp.exp(cum_last - cum)
        decay = jnp.exp(cum_last)
        st = st_ref[...]
        new_st = []
        for hh in range(GLA_HEADS):
            k0 = hh * GLA_DK
            v0 = hh * GLA_DV
            q_h = q_dec[:, k0:k0 + GLA_DK].astype(BF16)
            scores = _dot_nt(q_h, k_inv[:, k0:k0 + GLA_DK].astype(BF16))
            scores = jnp.where(causal, scores, 0.0).astype(BF16)
            o_h = _dot(scores, v16[r0:r0 + GLA_CHUNK, v0:v0 + GLA_DV])
            o_h = o_h + _dot_nt(q_h, st[:, k0:k0 + GLA_DK].astype(BF16))
            obuf[r0:r0 + GLA_CHUNK, v0:v0 + GLA_DV] = o_h
            upd = _dot(vt16[v0:v0 + GLA_DV, r0:r0 + GLA_CHUNK], k_end[:, k0:k0 + GLA_DK].astype(BF16))
            new_st.append(st[:, k0:k0 + GLA_DK] * decay[:, k0:k0 + GLA_DK] + upd)
        st_ref[...] = jnp.concatenate(new_st, axis=-1)

    @pl.when(t == pl.num_programs(1) - 1)
    def _():
        state_ref[...] = st_ref[...].T

    gate = proj[:, C_G:C_G + GLA_WIDTH]
    mixbuf[:, POOL_WIDTH:] = _head_norm_gate(obuf[...], gate, g_gla_ref[...]).astype(BF16)
    x1_ref[...] = x + _dot(mixbuf[...], w_out_ref[...])


def _mix_prompt(x2d, g_mix, w_in16, w_fu16, b_f, w_pool16, pool_scale, g_gla, w_out16):
    nt = SEQ // TM_MIX
    return pl.pallas_call(
        _mix_prompt_body,
        grid=(BATCH, nt),
        in_specs=[pl.BlockSpec((TM_MIX, D_MODEL), lambda b, t: (b * nt + t, 0)),
                  _const_spec((1, D_MODEL)), _const_spec((D_MODEL, IN_COLS_PAD)),
                  _const_spec((LANES, GLA_KEY_WIDTH)), _const_spec((1, GLA_KEY_WIDTH)),
                  _const_spec((len(POOL_WINDOWS), POOL_GROUP_WIDTH, POOL_GROUP_WIDTH)),
                  _const_spec((1, POOL_WIDTH)), _const_spec((1, GLA_WIDTH)),
                  _const_spec((D_MODEL, D_MODEL))],
        out_specs=[pl.BlockSpec((TM_MIX, D_MODEL), lambda b, t: (b * nt + t, 0)),
                   pl.BlockSpec((None, POOL_TAIL, POOL_WIDTH), lambda b, t: (b, 0, 0)),
                   pl.BlockSpec((None, GLA_KEY_WIDTH, GLA_DV), lambda b, t: (b, 0, 0))],
        out_shape=[jax.ShapeDtypeStruct((BATCH * SEQ, D_MODEL), F32),
                   jax.ShapeDtypeStruct((BATCH, POOL_TAIL, POOL_WIDTH), F32),
                   jax.ShapeDtypeStruct((BATCH, GLA_KEY_WIDTH, GLA_DV), F32)],
        scratch_shapes=[pltpu.VMEM((POOL_TAIL + TM_MIX, POOL_WIDTH), F32),
                        pltpu.VMEM((GLA_DV, GLA_KEY_WIDTH), F32),
                        pltpu.VMEM((TM_MIX, D_MODEL), BF16),
                        pltpu.VMEM((TM_MIX, GLA_WIDTH), F32)],
        compiler_params=pltpu.CompilerParams(dimension_semantics=("arbitrary", "arbitrary"),
                                             vmem_limit_bytes=VMEM_LIMIT),
        name="mix_prompt",
    )(x2d, g_mix, w_in16, w_fu16, b_f, w_pool16, pool_scale, g_gla, w_out16)


def _mem_query(x1, g_mem, w_qm_ref):
    return _dot(_rms(x1, g_mem).astype(BF16), w_qm_ref[...])


def _ffn_tail(x1, ctx16, w_om_ref, g_ffn, w_up_ref, w_down_ref, g_final):
    x2 = x1 + _dot(ctx16, w_om_ref[...])
    hf = _rms(x2, g_ffn).astype(BF16)
    acc = x2
    for f0 in range(0, D_FF, FF_CHUNK):
        a = jnp.maximum(_dot(hf, w_up_ref[:, f0:f0 + FF_CHUNK]), 0.0)
        acc = acc + _dot((a * a).astype(BF16), w_down_ref[f0:f0 + FF_CHUNK, :])
    return _rms(acc, g_final)


def _tail_prompt_body(x1_ref, mk_ref, mv_ref, g_mem_ref, w_qm_ref, w_om_ref, g_ffn_ref, w_up_ref, w_down_ref,
                      g_final_ref, y_ref, ctxbuf):
    x1 = x1_ref[...]
    qm = _mem_query(x1, g_mem_ref[...], w_qm_ref)
    for hh in range(MEM_HEADS):
        c0 = hh * MEM_HEAD_DIM
        s = _dot_nt(qm[:, c0:c0 + MEM_HEAD_DIM].astype(BF16), mk_ref[:, c0:c0 + MEM_HEAD_DIM])
        s = s * (MEM_HEAD_DIM ** -0.5)
        e = jnp.exp(s - jnp.max(s, axis=-1, keepdims=True))
        p = e / jnp.sum(e, axis=-1, keepdims=True)
        ctxbuf[:, c0:c0 + MEM_HEAD_DIM] = _dot(p.astype(BF16), mv_ref[:, c0:c0 + MEM_HEAD_DIM]).astype(BF16)
    y_ref[...] = _ffn_tail(x1, ctxbuf[...], w_om_ref, g_ffn_ref[...], w_up_ref, w_down_ref, g_final_ref[...])


def _tail_prompt(x1, mk16, mv16, g_mem, w_qm16, w_om16, g_ffn, w_up16, w_down16, g_final):
    n = x1.shape[0]
    tiles_per_seq = SEQ // TM_TAIL
    tok = pl.BlockSpec((TM_TAIL, D_MODEL), lambda i: (i, 0))
    mem = pl.BlockSpec((MEM_TOKENS, D_MODEL), lambda i: (i // tiles_per_seq, 0))
    return pl.pallas_call(
        _tail_prompt_body,
        grid=(n // TM_TAIL,),
        in_specs=[tok, mem, mem, _const_spec((1, D_MODEL)), _const_spec((D_MODEL, D_MODEL)),
                  _const_spec((D_MODEL, D_MODEL)), _const_spec((1, D_MODEL)), _const_spec((D_MODEL, D_FF)),
                  _const_spec((D_FF, D_MODEL)), _const_spec((1, D_MODEL))],
        out_specs=tok,
        out_shape=jax.ShapeDtypeStruct((n, D_MODEL), F32),
        scratch_shapes=[pltpu.VMEM((TM_TAIL, D_MODEL), BF16)],
        compiler_params=pltpu.CompilerParams(dimension_semantics=("arbitrary",), vmem_limit_bytes=VMEM_LIMIT),
        name="tail_prompt",
    )(x1, mk16, mv16, g_mem, w_qm16, w_om16, g_ffn, w_up16, w_down16, g_final)


def _mix_sample_body(x_ref, pool_ref, s0_ref, g_mix_ref, w_in_ref, w_fu_ref, b_f_ref, w_pool_ref, pool_scale_ref,
                     g_gla_ref, w_out_ref, x1_ref, pool_out_ref, s_out_ref,
                     qt_ref, kt_ref, at_ref, v_ref, o_ref, gate_ref, mixbuf):
    i = pl.program_id(0)
    nb = DEC_BATCH

    @pl.when(i == 0)
    def _():
        proj = _in_proj(x_ref[...], g_mix_ref[...], w_in_ref)
        u = proj[:, C_U:C_U + POOL_WIDTH]
        pooled = []
        for gi, w in enumerate(POOL_WINDOWS):
            c0 = gi * POOL_GROUP_WIDTH
            win = u[:, c0:c0 + POOL_GROUP_WIDTH]
            for j in range(1, w):
                l0 = (POOL_BUF - j) * POOL_WIDTH + c0
                win = win + pool_ref[:, l0:l0 + POOL_GROUP_WIDTH]
            pooled.append(win / float(w) - u[:, c0:c0 + POOL_GROUP_WIDTH])
        mixbuf[:, 0:POOL_WIDTH] = _pool_project(pooled, w_pool_ref, pool_scale_ref[...]).astype(BF16)
        pool_out_ref[:, 0:(POOL_BUF - 1) * POOL_WIDTH] = pool_ref[:, POOL_WIDTH:]
        pool_out_ref[:, (POOL_BUF - 1) * POOL_WIDTH:] = u

        log_f = _log_forget(proj[:, C_F:C_F + LANES], w_fu_ref, b_f_ref[...])
        alpha = jnp.exp(log_f)
        q_dec = proj[:, C_Q:C_Q + GLA_KEY_WIDTH] * (GLA_DK ** -0.5) * alpha
        k = proj[:, C_K:C_K + GLA_KEY_WIDTH]
        k_inv = k * jnp.exp(-log_f)
        v = proj[:, C_V:C_V + GLA_WIDTH]
        qk = q_dec * k_inv
        o_intra = []
        for hh in range(GLA_HEADS):
            s_h = jnp.sum(qk[:, hh * GLA_DK:(hh + 1) * GLA_DK], axis=-1, keepdims=True)
            o_intra.append(s_h * v[:, hh * GLA_DV:(hh + 1) * GLA_DV])
        o_ref[...] = jnp.concatenate(o_intra, axis=-1)
        v_ref[...] = v
        gate_ref[...] = proj[:, C_G:C_G + GLA_WIDTH]
        qt_ref[...] = q_dec.T
        kt_ref[...] = k.T
        at_ref[...] = alpha.T

    shift = (nb - i * SB_MIX) % nb
    qt = pltpu.roll(qt_ref[...], shift, axis=1)
    kt = pltpu.roll(kt_ref[...], shift, axis=1)
    at = pltpu.roll(at_ref[...], shift, axis=1)
    b0 = pl.multiple_of(i * SB_MIX, SB_MIX)
    v_blk = v_ref[pl.ds(b0, SB_MIX), :]
    o_rows = []
    for j in range(SB_MIX):
        o_heads = []
        for hh in range(GLA_HEADS):
            k0 = hh * GLA_DK
            v0 = hh * GLA_DV
            s_old = s0_ref[j, k0:k0 + GLA_DK, :]
            q_col = jnp.broadcast_to(qt[k0:k0 + GLA_DK, j:j + 1], (GLA_DK, GLA_DV))
            k_col = jnp.broadcast_to(kt[k0:k0 + GLA_DK, j:j + 1], (GLA_DK, GLA_DV))
            a_col = jnp.broadcast_to(at[k0:k0 + GLA_DK, j:j + 1], (GLA_DK, GLA_DV))
            v_row = v_blk[j:j + 1, v0:v0 + GLA_DV]
            s_out_ref[j, k0:k0 + GLA_DK, :] = a_col * s_old + k_col * v_row
            o_heads.append(jnp.sum(q_col * s_old, axis=0, keepdims=True))
        o_rows.append(jnp.concatenate(o_heads, axis=-1))
    o_ref[pl.ds(b0, SB_MIX), :] = o_ref[pl.ds(b0, SB_MIX), :] + jnp.concatenate(o_rows, axis=0)

    @pl.when(i == pl.num_programs(0) - 1)
    def _():
        mixbuf[:, POOL_WIDTH:] = _head_norm_gate(o_ref[...], gate_ref[...], g_gla_ref[...]).astype(BF16)
        x1_ref[...] = x_ref[...] + _dot(mixbuf[...], w_out_ref[...])


def _mix_sample(xs, pool2d, s0, g_mix, w_in16, w_fu16, b_f, w_pool16, pool_scale, g_gla, w_out16):
    nb = DEC_BATCH
    state_blk = pl.BlockSpec((SB_MIX, GLA_KEY_WIDTH, GLA_DV), lambda i: (i, 0, 0))
    return pl.pallas_call(
        _mix_sample_body,
        grid=(nb // SB_MIX,),
        in_specs=[_const_spec((nb, D_MODEL)), _const_spec((nb, POOL_BUF * POOL_WIDTH)), state_blk,
                  _const_spec((1, D_MODEL)), _const_spec((D_MODEL, IN_COLS_PAD)),
                  _const_spec((LANES, GLA_KEY_WIDTH)), _const_spec((1, GLA_KEY_WIDTH)),
                  _const_spec((len(POOL_WINDOWS), POOL_GROUP_WIDTH, POOL_GROUP_WIDTH)),
                  _const_spec((1, POOL_WIDTH)), _const_spec((1, GLA_WIDTH)),
                  _const_spec((D_MODEL, D_MODEL))],
        out_specs=[pl.BlockSpec((nb, D_MODEL), lambda i: (0, 0)),
                   pl.BlockSpec((nb, POOL_BUF * POOL_WIDTH), lambda i: (0, 0)),
                   state_blk],
        out_shape=[jax.ShapeDtypeStruct((nb, D_MODEL), F32),
                   jax.ShapeDtypeStruct((nb, POOL_BUF * POOL_WIDTH), F32),
                   jax.ShapeDtypeStruct((nb, GLA_KEY_WIDTH, GLA_DV), F32)],
        scratch_shapes=[pltpu.VMEM((GLA_KEY_WIDTH, nb), F32), pltpu.VMEM((GLA_KEY_WIDTH, nb), F32),
                        pltpu.VMEM((GLA_KEY_WIDTH, nb), F32), pltpu.VMEM((nb, GLA_WIDTH), F32),
                        pltpu.VMEM((nb, GLA_WIDTH), F32), pltpu.VMEM((nb, GLA_WIDTH), F32),
                        pltpu.VMEM((nb, D_MODEL), BF16)],
        compiler_params=pltpu.CompilerParams(dimension_semantics=("arbitrary",), vmem_limit_bytes=VMEM_LIMIT),
        name="mix_sample",
    )(xs, pool2d, s0, g_mix, w_in16, w_fu16, b_f, w_pool16, pool_scale, g_gla, w_out16)


def _query_sample_body(x1_ref, g_mem_ref, w_qm_ref, q_ref):
    q_ref[...] = _mem_query(x1_ref[...], g_mem_ref[...], w_qm_ref) * (MEM_HEAD_DIM ** -0.5)


def _query_sample(x1, g_mem, w_qm16):
    nb = DEC_BATCH
    return pl.pallas_call(
        _query_sample_body,
        grid=(1,),
        in_specs=[_const_spec((nb, D_MODEL)), _const_spec((1, D_MODEL)), _const_spec((D_MODEL, D_MODEL))],
        out_specs=pl.BlockSpec((nb, D_MODEL), lambda i: (0, 0)),
        out_shape=jax.ShapeDtypeStruct((nb, D_MODEL), F32),
        compiler_params=pltpu.CompilerParams(dimension_semantics=("arbitrary",), vmem_limit_bytes=VMEM_LIMIT),
        name="query_sample",
    )(x1, g_mem, w_qm16)


def _attn_sample_body(q_ref, ck_ref, cv_ref, ctx_ref):
    for j in range(SB_ATT):
        s = jnp.sum(ck_ref[j] * q_ref[j][None], axis=-1, keepdims=True)
        e = jnp.exp(s - jnp.max(s, axis=0, keepdims=True))
        p = e / jnp.sum(e, axis=0, keepdims=True)
        ctx_ref[j] = jnp.sum(p * cv_ref[j], axis=0)


def _attn_sample(q3, ck, cv):
    nb = DEC_BATCH
    cache = pl.BlockSpec((SB_ATT, MEM_TOKENS, MEM_HEADS, MEM_HEAD_DIM), lambda i: (i, 0, 0, 0))
    qblk = pl.BlockSpec((SB_ATT, MEM_HEADS, MEM_HEAD_DIM), lambda i: (i, 0, 0))
    return pl.pallas_call(
        _attn_sample_body,
        grid=(nb // SB_ATT,),
        in_specs=[qblk, cache, cache],
        out_specs=qblk,
        out_shape=jax.ShapeDtypeStruct((nb, MEM_HEADS, MEM_HEAD_DIM), F32),
        compiler_params=pltpu.CompilerParams(dimension_semantics=("arbitrary",), vmem_limit_bytes=VMEM_LIMIT),
        name="attn_sample",
    )(q3, ck, cv)


def _ffn_sample_body(x1_ref, ctx_ref, w_om_ref, g_ffn_ref, w_up_ref, w_down_ref, g_final_ref, y_ref):
    y_ref[...] = _ffn_tail(x1_ref[...], ctx_ref[...].astype(BF16), w_om_ref, g_ffn_ref[...], w_up_ref,
                           w_down_ref, g_final_ref[...])


def _ffn_sample(x1, ctx, w_om16, g_ffn, w_up16, w_down16, g_final):
    nb = DEC_BATCH
    return pl.pallas_call(
        _ffn_sample_body,
        grid=(1,),
        in_specs=[_const_spec((nb, D_MODEL)), _const_spec((nb, D_MODEL)), _const_spec((D_MODEL, D_MODEL)),
                  _const_spec((1, D_MODEL)), _const_spec((D_MODEL, D_FF)), _const_spec((D_FF, D_MODEL)),
                  _const_spec((1, D_MODEL))],
        out_specs=pl.BlockSpec((nb, D_MODEL), lambda i: (0, 0)),
        out_shape=jax.ShapeDtypeStruct((nb, D_MODEL), F32),
        compiler_params=pltpu.CompilerParams(dimension_semantics=("arbitrary",), vmem_limit_bytes=VMEM_LIMIT),
        name="ffn_sample",
    )(x1, ctx, w_om16, g_ffn, w_up16, w_down16, g_final)


def kernel(x_prompt, x_sample, state_pool, state_gla, cache_mem_k, cache_mem_v, mem_prompt, norm_mix_g, w_in,
           w_forget_up, b_forget, w_pool, pool_scale, gla_norm_g, w_out, mem_norm_g, w_km, w_vm, norm_mem_g, w_qm,
           w_om, norm_ffn_g, w_up, w_down, norm_final_g):
    assert x_prompt.shape == (BATCH, SEQ, D_MODEL) and x_sample.shape == (DEC_BATCH, 1, D_MODEL)
    assert w_in.shape[0] == 1, "single layer"
    pad_cols = IN_COLS_PAD - w_in.shape[-1]
    w_in16 = jnp.pad(w_in[0], ((0, 0), (0, pad_cols))).astype(BF16)
    w_fu16 = jnp.pad(w_forget_up[0], ((0, LANES - GLA_GATE_RANK), (0, 0))).astype(BF16)
    w_pool16 = w_pool[0].astype(BF16)
    w_out16 = w_out[0].astype(BF16)
    w_kv16 = jnp.concatenate([w_km[0], w_vm[0]], axis=1).astype(BF16)
    w_qm16 = w_qm[0].astype(BF16)
    w_om16 = w_om[0].astype(BF16)
    w_up16 = w_up[0].astype(BF16)
    w_down16 = w_down[0].astype(BF16)
    g_mix = norm_mix_g.reshape(1, D_MODEL)
    g_memn = mem_norm_g.reshape(1, D_MODEL)
    g_mem = norm_mem_g.reshape(1, D_MODEL)
    g_ffn = norm_ffn_g.reshape(1, D_MODEL)
    g_final = norm_final_g.reshape(1, D_MODEL)
    g_gla = gla_norm_g.reshape(1, GLA_WIDTH)
    b_f = b_forget.reshape(1, GLA_KEY_WIDTH)
    p_scale = pool_scale.reshape(1, POOL_WIDTH)

    mk, mv, mk16, mv16 = _mem_kv(mem_prompt.reshape(BATCH * MEM_TOKENS, D_MODEL), g_memn, w_kv16)
    x1p, tail_p, gla_p = _mix_prompt(x_prompt.reshape(BATCH * SEQ, D_MODEL), g_mix, w_in16, w_fu16, b_f, w_pool16,
                                     p_scale, g_gla, w_out16)
    y_p = _tail_prompt(x1p, mk16, mv16, g_mem, w_qm16, w_om16, g_ffn, w_up16, w_down16, g_final)

    x1s, pool_s, gla_s = _mix_sample(x_sample.reshape(DEC_BATCH, D_MODEL),
                                     state_pool.reshape(DEC_BATCH, POOL_BUF * POOL_WIDTH),
                                     state_gla.reshape(DEC_BATCH, GLA_KEY_WIDTH, GLA_DV), g_mix, w_in16, w_fu16,
                                     b_f, w_pool16, p_scale, g_gla, w_out16)
    cache_shape = (DEC_BATCH, MEM_TOKENS, MEM_HEADS, MEM_HEAD_DIM)
    q_s = _query_sample(x1s, g_mem, w_qm16).reshape(DEC_BATCH, MEM_HEADS, MEM_HEAD_DIM)
    ctx_s = _attn_sample(q_s, cache_mem_k.reshape(cache_shape), cache_mem_v.reshape(cache_shape))
    y_s = _ffn_sample(x1s, ctx_s.reshape(DEC_BATCH, D_MODEL), w_om16, g_ffn, w_up16, w_down16, g_final)

    kv_shape = (1, BATCH, MEM_TOKENS, MEM_HEADS, MEM_HEAD_DIM)
    return (y_p.reshape(BATCH, SEQ, D_MODEL),
            y_s.reshape(DEC_BATCH, 1, D_MODEL),
            tail_p[:, POOL_TAIL - POOL_BUF:, :].reshape(1, BATCH, POOL_BUF, POOL_WIDTH),
            gla_p.reshape(1, BATCH, GLA_HEADS, GLA_DK, GLA_DV),
            mk.reshape(kv_shape),
            mv.reshape(kv_shape),
            pool_s.reshape(1, DEC_BATCH, POOL_BUF, POOL_WIDTH),
            gla_s.reshape(1, DEC_BATCH, GLA_HEADS, GLA_DK, GLA_DV))
```

```python
import functools

import jax
import jax.numpy as jnp
from jax import lax
from jax.experimental import pallas as pl
from jax.experimental.pallas import tpu as pltpu

F32 = jnp.float32
BF16 = jnp.bfloat16

D_MODEL = 1024
BATCH = 8
SEQ = 2048
DEC_BATCH = 128
POOL_WIDTH = 512
POOL_WINDOWS = (2, 4, 8, 16)
POOL_GROUP_WIDTH = 128
POOL_BUF = 15
POOL_TAIL = 16
GLA_WIDTH = 512
GLA_HEADS = 4
GLA_DV = 128
GLA_DK = 64
GLA_KEY_WIDTH = 256
GLA_GATE_RANK = 16
GLA_GATE_NORM = 16.0
GLA_CHUNK = 64
MEM_TOKENS = 256
MEM_HEADS = 4
MEM_HEAD_DIM = 256
D_FF = 4096
EPS = 1e-6

LANES = 128
IN_COLS_PAD = 2048 + LANES
C_U, C_Q, C_K, C_V, C_G, C_F = 0, 512, 768, 1024, 1536, 2048

VMEM_LIMIT = 56 * 1024 * 1024

TM_MIX = 512
SUB_MIX = 256
IN_PARTS = ((C_U, C_Q), (C_Q, C_V), (C_V, C_G), (C_G, IN_COLS_PAD))
OUT_PART = D_MODEL // len(IN_PARTS)
TM_TAIL = 512
FF_CHUNK = 1024
SB_MIX = 8
SB_ATT = 4


def _rms(x, g):
    return x * lax.rsqrt(jnp.mean(x * x, axis=-1, keepdims=True) + EPS) * g


def _dot(a, b):
    return jnp.dot(a, b, preferred_element_type=F32)


def _dot_nt(a, b):
    return lax.dot_general(a, b, (((1,), (1,)), ((), ())), preferred_element_type=F32)


def _log_sigmoid(x):
    return jnp.minimum(x, 0.0) - jnp.log1p(jnp.exp(-jnp.abs(x)))


def _silu(x):
    return x / (1.0 + jnp.exp(-x))


def _const_spec(shape):
    zeros = (0,) * len(shape)
    return pl.BlockSpec(shape, lambda *_: zeros, pipeline_mode=pl.Buffered(1))


def _in_proj(x, g_mix, w_in_ref):
    h = _rms(x, g_mix).astype(BF16)
    return _dot(h, w_in_ref[...])


def _log_forget(f_low, w_fu_ref, b_f):
    pre = _dot(f_low.astype(BF16), w_fu_ref[...]) + b_f
    return _log_sigmoid(pre) * (1.0 / GLA_GATE_NORM)


def _pool_project(pooled, w_pool_ref, pool_scale):
    outs = []
    for gi in range(len(POOL_WINDOWS)):
        c0 = gi * POOL_GROUP_WIDTH
        outs.append(_dot(pooled[gi].astype(BF16), w_pool_ref[gi]) * pool_scale[:, c0:c0 + POOL_GROUP_WIDTH])
    return jnp.concatenate(outs, axis=-1)


def _head_norm_gate(o, gate, g_gla):
    outs = []
    for hh in range(GLA_HEADS):
        c0 = hh * GLA_DV
        outs.append(_rms(o[:, c0:c0 + GLA_DV], g_gla[:, c0:c0 + GLA_DV]))
    return jnp.concatenate(outs, axis=-1) * _silu(gate)


def _memkv_body(mem_ref, g_ref, w_ref, mk_ref, mv_ref, mk16_ref, mv16_ref):
    mn = _rms(mem_ref[...], g_ref[...]).astype(BF16)
    kv = _dot(mn, w_ref[...])
    k = kv[:, :D_MODEL]
    v = kv[:, D_MODEL:]
    mk_ref[...] = k
    mv_ref[...] = v
    mk16_ref[...] = k.astype(BF16)
    mv16_ref[...] = v.astype(BF16)


def _mem_kv(mem2d, g_mem, w_kv16):
    n = mem2d.shape[0]
    blk = pl.BlockSpec((MEM_TOKENS, D_MODEL), lambda b: (b, 0))
    return pl.pallas_call(
        _memkv_body,
        grid=(n // MEM_TOKENS,),
        in_specs=[blk, _const_spec((1, D_MODEL)), _const_spec((D_MODEL, 2 * D_MODEL))],
        out_specs=[blk, blk, blk, blk],
        out_shape=[jax.ShapeDtypeStruct((n, D_MODEL), F32), jax.ShapeDtypeStruct((n, D_MODEL), F32),
                   jax.ShapeDtypeStruct((n, D_MODEL), BF16), jax.ShapeDtypeStruct((n, D_MODEL), BF16)],
        compiler_params=pltpu.CompilerParams(dimension_semantics=("arbitrary",), vmem_limit_bytes=VMEM_LIMIT),
        name="mem_kv",
    )(mem2d, g_mem, w_kv16)


def _mix_prompt_body(x_ref, g_mix_ref, w_in_ref, w_fu_ref, b_f_ref, w_pool_ref, pool_scale_ref, g_gla_ref,
                     w_out_ref, x1_ref, tail_ref, state_ref, ubuf, st_ref, mixbuf):
    t = pl.program_id(1)
    tm = TM_MIX
    n_sub = tm // SUB_MIX

    @pl.when(t == 0)
    def _():
        ubuf[0:POOL_TAIL, :] = jnp.zeros((POOL_TAIL, POOL_WIDTH), F32)
        st_ref[...] = jnp.zeros(st_ref.shape, F32)

    n_chunk = SUB_MIX // GLA_CHUNK
    g_mix = g_mix_ref[...]
    xs = [x_ref[s * SUB_MIX:(s + 1) * SUB_MIX, :] for s in range(n_sub)]
    hs = [_rms(x, g_mix).astype(BF16) for x in xs]

    def proj_part(s, j):
        c0, c1 = IN_PARTS[j]
        return _dot(hs[s], w_in_ref[:, c0:c1])

    def out_part(s, j):
        r0, c0 = s * SUB_MIX, j * OUT_PART
        x1_ref[r0:r0 + SUB_MIX, c0:c0 + OUT_PART] = xs[s][:, c0:c0 + OUT_PART] + _dot(
            mixbuf[r0:r0 + SUB_MIX, :], w_out_ref[:, c0:c0 + OUT_PART])

    row = lax.broadcasted_iota(jnp.int32, (SUB_MIX, SUB_MIX), 0)
    col = lax.broadcasted_iota(jnp.int32, (SUB_MIX, SUB_MIX), 1)
    chunk_causal = ((row // GLA_CHUNK) == (col // GLA_CHUNK)) & (row >= col)
    tril16 = jnp.where(chunk_causal, 1.0, 0.0).astype(BF16)
    lane_head = lax.broadcasted_iota(jnp.int32, (SUB_MIX, GLA_KEY_WIDTH), 1) // GLA_DK
    row_chunk = lax.broadcasted_iota(jnp.int32, (SUB_MIX, GLA_DV), 0) // GLA_CHUNK
    row1 = 1 + lax.broadcasted_iota(jnp.int32, (SUB_MIX, POOL_GROUP_WIDTH), 0)

    parts = [proj_part(0, j) for j in range(len(IN_PARTS))]
    st = st_ref[...]
    for s in range(n_sub):
        r_sub = s * SUB_MIX
        p_u, p_qk, p_v, p_gf = parts
        parts = []

        def interleave(j):
            if s + 1 < n_sub:
                parts.append(proj_part(s + 1, j))
            if s >= 1:
                out_part(s - 1, j)

        ubuf[POOL_TAIL + r_sub:POOL_TAIL + r_sub + SUB_MIX, :] = p_u
        q = p_qk[:, 0:GLA_KEY_WIDTH] * (GLA_DK ** -0.5)
        k = p_qk[:, GLA_KEY_WIDTH:]
        gate = p_gf[:, 0:GLA_WIDTH]

        log_f = _log_forget(p_gf[:, GLA_WIDTH:], w_fu_ref, b_f_ref[...])
        f_hi = log_f.astype(BF16)
        rem = log_f - f_hi.astype(F32)
        f_mid = rem.astype(BF16)
        f_lo = (rem - f_mid.astype(F32)).astype(BF16)
        cum3 = _dot(tril16, jnp.concatenate([f_hi, f_mid, f_lo], axis=-1))
        cum = cum3[:, 0:GLA_KEY_WIDTH] + cum3[:, GLA_KEY_WIDTH:2 * GLA_KEY_WIDTH] + cum3[:, 2 * GLA_KEY_WIDTH:]
        interleave(0)

        pos1 = t * tm + r_sub + row1
        pooled = []
        for gi, w in enumerate(POOL_WINDOWS):
            c0 = gi * POOL_GROUP_WIDTH
            win = ubuf[r_sub:r_sub + POOL_TAIL + SUB_MIX, c0:c0 + POOL_GROUP_WIDTH]
            span = 1
            while span < w:
                win = win + pltpu.roll(win, span, axis=0)
                span *= 2
            count = jnp.minimum(pos1, w).astype(F32)
            pooled.append(win[POOL_TAIL:, :] / count - p_u[:, c0:c0 + POOL_GROUP_WIDTH])
        mixbuf[r_sub:r_sub + SUB_MIX, 0:POOL_WIDTH] = _pool_project(
            pooled, w_pool_ref, pool_scale_ref[...]).astype(BF16)

        cum_last = jnp.concatenate(
            [jnp.broadcast_to(cum[(c + 1) * GLA_CHUNK - 1:(c + 1) * GLA_CHUNK, :], (GLA_CHUNK, GLA_KEY_WIDTH))
             for c in range(n_chunk)], axis=0)
        q_dec = (q * jnp.exp(cum)).astype(BF16)
        k_inv = (k * jnp.exp(-cum)).astype(BF16)
        k_end_t = (k * jnp.exp(cum_last - cum)).T.astype(BF16)
        decay_t = jnp.exp(cum.T)
        v16 = p_v.astype(BF16)
        q_heads = [jnp.where(lane_head == hh, q_dec, jnp.zeros_like(q_dec)) for hh in range(GLA_HEADS)]
        interleave(1)

        probs = []
        for hh in range(GLA_HEADS):
            scores = _dot_nt(q_heads[hh], k_inv)
            probs.append(jnp.where(chunk_causal, scores, 0.0).astype(BF16))
        upd = []
        for c in range(n_chunk):
            rows = []
            for hh in range(GLA_HEADS):
                v_ch = v16[:, hh * GLA_DV:(hh + 1) * GLA_DV]
                v_ch = jnp.where(row_chunk == c, v_ch, jnp.zeros_like(v_ch))
                rows.append(_dot(k_end_t[hh * GLA_DK:(hh + 1) * GLA_DK, :], v_ch))
            upd.append(jnp.concatenate(rows, axis=0))
        interleave(2)

        o_intra = [_dot(probs[hh], v16[:, hh * GLA_DV:(hh + 1) * GLA_DV]) for hh in range(GLA_HEADS)]
        states = []
        for c in range(n_chunk):
            states.append(st.astype(BF16))
            c_end = (c + 1) * GLA_CHUNK
            st = st * jnp.broadcast_to(decay_t[:, c_end - 1:c_end], (GLA_KEY_WIDTH, GLA_DV)) + upd[c]
        interleave(3)

        for c in range(n_chunk):
            r0 = c * GLA_CHUNK
            o_heads = [o_intra[hh][r0:r0 + GLA_CHUNK, :] + _dot(q_heads[hh][r0:r0 + GLA_CHUNK, :], states[c])
                       for hh in range(GLA_HEADS)]
            mixbuf[r_sub + r0:r_sub + r0 + GLA_CHUNK, POOL_WIDTH:] = _head_norm_gate(
                jnp.concatenate(o_heads, axis=-1), gate[r0:r0 + GLA_CHUNK, :], g_gla_ref[...]).astype(BF16)

    for j in range(D_MODEL // OUT_PART):
        out_part(n_sub - 1, j)

    st_ref[...] = st
    tail = ubuf[tm:tm + POOL_TAIL, :]
    ubuf[0:POOL_TAIL, :] = tail
    tail_ref[...] = tail

    @pl.when(t == pl.num_programs(1) - 1)
    def _():
        state_ref[...] = st


def _mix_prompt(x2d, g_mix, w_in16, w_fu16, b_f, w_pool16, pool_scale, g_gla, w_out16):
    nt = SEQ // TM_MIX
    return pl.pallas_call(
        _mix_prompt_body,
        grid=(BATCH, nt),
        in_specs=[pl.BlockSpec((TM_MIX, D_MODEL), lambda b, t: (b * nt + t, 0)),
                  _const_spec((1, D_MODEL)), _const_spec((D_MODEL, IN_COLS_PAD)),
                  _const_spec((LANES, GLA_KEY_WIDTH)), _const_spec((1, GLA_KEY_WIDTH)),
                  _const_spec((len(POOL_WINDOWS), POOL_GROUP_WIDTH, POOL_GROUP_WIDTH)),
                  _const_spec((1, POOL_WIDTH)), _const_spec((1, GLA_WIDTH)),
                  _const_spec((D_MODEL, D_MODEL))],
        out_specs=[pl.BlockSpec((TM_MIX, D_MODEL), lambda b, t: (b * nt + t, 0)),
                   pl.BlockSpec((None, POOL_TAIL, POOL_WIDTH), lambda b, t: (b, 0, 0)),
                   pl.BlockSpec((None, GLA_KEY_WIDTH, GLA_DV), lambda b, t: (b, 0, 0))],
        out_shape=[jax.ShapeDtypeStruct((BATCH * SEQ, D_MODEL), F32),
                   jax.ShapeDtypeStruct((BATCH, POOL_TAIL, POOL_WIDTH), F32),
                   jax.ShapeDtypeStruct((BATCH, GLA_KEY_WIDTH, GLA_DV), F32)],
        scratch_shapes=[pltpu.VMEM((POOL_TAIL + TM_MIX, POOL_WIDTH), F32),
                        pltpu.VMEM((GLA_KEY_WIDTH, GLA_DV), F32),
                        pltpu.VMEM((TM_MIX, D_MODEL), BF16)],
        compiler_params=pltpu.CompilerParams(dimension_semantics=("arbitrary", "arbitrary"),
                                             vmem_limit_bytes=VMEM_LIMIT),
        name="mix_prompt",
    )(x2d, g_mix, w_in16, w_fu16, b_f, w_pool16, pool_scale, g_gla, w_out16)


def _mem_query(x1, g_mem, w_qm_ref):
    return _dot(_rms(x1, g_mem).astype(BF16), w_qm_ref[...])


def _ffn_tail(x1, ctx16, w_om_ref, g_ffn, w_up_ref, w_down_ref, g_final):
    x2 = x1 + _dot(ctx16, w_om_ref[...])
    hf = _rms(x2, g_ffn).astype(BF16)
    acc = x2
    for f0 in range(0, D_FF, FF_CHUNK):
        a = jnp.maximum(_dot(hf, w_up_ref[:, f0:f0 + FF_CHUNK]), 0.0)
        acc = acc + _dot((a * a).astype(BF16), w_down_ref[f0:f0 + FF_CHUNK, :])
    return _rms(acc, g_final)


def _tail_prompt_body(x1_ref, mk_ref, mv_ref, g_mem_ref, w_qm_ref, w_om_ref, g_ffn_ref, w_up_ref, w_down_ref,
                      g_final_ref, y_ref, ctxbuf):
    x1 = x1_ref[...]
    qm = _mem_query(x1, g_mem_ref[...], w_qm_ref)
    for hh in range(MEM_HEADS):
        c0 = hh * MEM_HEAD_DIM
        s = _dot_nt(qm[:, c0:c0 + MEM_HEAD_DIM].astype(BF16), mk_ref[:, c0:c0 + MEM_HEAD_DIM])
        s = s * (MEM_HEAD_DIM ** -0.5)
        e = jnp.exp(s - jnp.max(s, axis=-1, keepdims=True))
        p = e / jnp.sum(e, axis=-1, keepdims=True)
        ctxbuf[:, c0:c0 + MEM_HEAD_DIM] = _dot(p.astype(BF16), mv_ref[:, c0:c0 + MEM_HEAD_DIM]).astype(BF16)
    y_ref[...] = _ffn_tail(x1, ctxbuf[...], w_om_ref, g_ffn_ref[...], w_up_ref, w_down_ref, g_final_ref[...])


def _tail_prompt(x1, mk16, mv16, g_mem, w_qm16, w_om16, g_ffn, w_up16, w_down16, g_final):
    n = x1.shape[0]
    tiles_per_seq = SEQ // TM_TAIL
    tok = pl.BlockSpec((TM_TAIL, D_MODEL), lambda i: (i, 0))
    mem = pl.BlockSpec((MEM_TOKENS, D_MODEL), lambda i: (i // tiles_per_seq, 0))
    return pl.pallas_call(
        _tail_prompt_body,
        grid=(n // TM_TAIL,),
        in_specs=[tok, mem, mem, _const_spec((1, D_MODEL)), _const_spec((D_MODEL, D_MODEL)),
                  _const_spec((D_MODEL, D_MODEL)), _const_spec((1, D_MODEL)), _const_spec((D_MODEL, D_FF)),
                  _const_spec((D_FF, D_MODEL)), _const_spec((1, D_MODEL))],
        out_specs=tok,
        out_shape=jax.ShapeDtypeStruct((n, D_MODEL), F32),
        scratch_shapes=[pltpu.VMEM((TM_TAIL, D_MODEL), BF16)],
        compiler_params=pltpu.CompilerParams(dimension_semantics=("arbitrary",), vmem_limit_bytes=VMEM_LIMIT),
        name="tail_prompt",
    )(x1, mk16, mv16, g_mem, w_qm16, w_om16, g_ffn, w_up16, w_down16, g_final)


def _mix_sample_body(x_ref, pool_ref, s0_ref, g_mix_ref, w_in_ref, w_fu_ref, b_f_ref, w_pool_ref, pool_scale_ref,
                     g_gla_ref, w_out_ref, x1_ref, pool_out_ref, s_out_ref,
                     qt_ref, kt_ref, at_ref, v_ref, o_ref, gate_ref, mixbuf):
    i = pl.program_id(0)
    nb = DEC_BATCH

    @pl.when(i == 0)
    def _():
        proj = _in_proj(x_ref[...], g_mix_ref[...], w_in_ref)
        u = proj[:, C_U:C_U + POOL_WIDTH]
        pooled = []
        for gi, w in enumerate(POOL_WINDOWS):
            c0 = gi * POOL_GROUP_WIDTH
            win = u[:, c0:c0 + POOL_GROUP_WIDTH]
            for j in range(1, w):
                l0 = (POOL_BUF - j) * POOL_WIDTH + c0
                win = win + pool_ref[:, l0:l0 + POOL_GROUP_WIDTH]
            pooled.append(win / float(w) - u[:, c0:c0 + POOL_GROUP_WIDTH])
        mixbuf[:, 0:POOL_WIDTH] = _pool_project(pooled, w_pool_ref, pool_scale_ref[...]).astype(BF16)
        pool_out_ref[:, 0:(POOL_BUF - 1) * POOL_WIDTH] = pool_ref[:, POOL_WIDTH:]
        pool_out_ref[:, (POOL_BUF - 1) * POOL_WIDTH:] = u

        log_f = _log_forget(proj[:, C_F:C_F + LANES], w_fu_ref, b_f_ref[...])
        alpha = jnp.exp(log_f)
        q_dec = proj[:, C_Q:C_Q + GLA_KEY_WIDTH] * (GLA_DK ** -0.5) * alpha
        k = proj[:, C_K:C_K + GLA_KEY_WIDTH]
        k_inv = k * jnp.exp(-log_f)
        v = proj[:, C_V:C_V + GLA_WIDTH]
        qk = q_dec * k_inv
        o_intra = []
        for hh in range(GLA_HEADS):
            s_h = jnp.sum(qk[:, hh * GLA_DK:(hh + 1) * GLA_DK], axis=-1, keepdims=True)
            o_intra.append(s_h * v[:, hh * GLA_DV:(hh + 1) * GLA_DV])
        o_ref[...] = jnp.concatenate(o_intra, axis=-1)
        v_ref[...] = v
        gate_ref[...] = proj[:, C_G:C_G + GLA_WIDTH]
        qt_ref[...] = q_dec.T
        kt_ref[...] = k.T
        at_ref[...] = alpha.T

    shift = (nb - i * SB_MIX) % nb
    qt = pltpu.roll(qt_ref[...], shift, axis=1)
    kt = pltpu.roll(kt_ref[...], shift, axis=1)
    at = pltpu.roll(at_ref[...], shift, axis=1)
    b0 = pl.multiple_of(i * SB_MIX, SB_MIX)
    v_blk = v_ref[pl.ds(b0, SB_MIX), :]
    o_rows = []
    for j in range(SB_MIX):
        o_heads = []
        for hh in range(GLA_HEADS):
            k0 = hh * GLA_DK
            v0 = hh * GLA_DV
            s_old = s0_ref[j, k0:k0 + GLA_DK, :]
            q_col = jnp.broadcast_to(qt[k0:k0 + GLA_DK, j:j + 1], (GLA_DK, GLA_DV))
            k_col = jnp.broadcast_to(kt[k0:k0 + GLA_DK, j:j + 1], (GLA_DK, GLA_DV))
            a_col = jnp.broadcast_to(at[k0:k0 + GLA_DK, j:j + 1], (GLA_DK, GLA_DV))
            v_row = v_blk[j:j + 1, v0:v0 + GLA_DV]
            s_out_ref[j, k0:k0 + GLA_DK, :] = a_col * s_old + k_col * v_row
            o_heads.append(jnp.sum(q_col * s_old, axis=0, keepdims=True))
        o_rows.append(jnp.concatenate(o_heads, axis=-1))
    o_ref[pl.ds(b0, SB_MIX), :] = o_ref[pl.ds(b0, SB_MIX), :] + jnp.concatenate(o_rows, axis=0)

    @pl.when(i == pl.num_programs(0) - 1)
    def _():
        mixbuf[:, POOL_WIDTH:] = _head_norm_gate(o_ref[...], gate_ref[...], g_gla_ref[...]).astype(BF16)
        x1_ref[...] = x_ref[...] + _dot(mixbuf[...], w_out_ref[...])


def _mix_sample(xs, pool2d, s0, g_mix, w_in16, w_fu16, b_f, w_pool16, pool_scale, g_gla, w_out16):
    nb = DEC_BATCH
    state_blk = pl.BlockSpec((SB_MIX, GLA_KEY_WIDTH, GLA_DV), lambda i: (i, 0, 0))
    return pl.pallas_call(
        _mix_sample_body,
        grid=(nb // SB_MIX,),
        in_specs=[_const_spec((nb, D_MODEL)), _const_spec((nb, POOL_BUF * POOL_WIDTH)), state_blk,
                  _const_spec((1, D_MODEL)), _const_spec((D_MODEL, IN_COLS_PAD)),
                  _const_spec((LANES, GLA_KEY_WIDTH)), _const_spec((1, GLA_KEY_WIDTH)),
                  _const_spec((len(POOL_WINDOWS), POOL_GROUP_WIDTH, POOL_GROUP_WIDTH)),
                  _const_spec((1, POOL_WIDTH)), _const_spec((1, GLA_WIDTH)),
                  _const_spec((D_MODEL, D_MODEL))],
        out_specs=[pl.BlockSpec((nb, D_MODEL), lambda i: (0, 0)),
                   pl.BlockSpec((nb, POOL_BUF * POOL_WIDTH), lambda i: (0, 0)),
                   state_blk],
        out_shape=[jax.ShapeDtypeStruct((nb, D_MODEL), F32),
                   jax.ShapeDtypeStruct((nb, POOL_BUF * POOL_WIDTH), F32),
                   jax.ShapeDtypeStruct((nb, GLA_KEY_WIDTH, GLA_DV), F32)],
        scratch_shapes=[pltpu.VMEM((GLA_KEY_WIDTH, nb), F32), pltpu.VMEM((GLA_KEY_WIDTH, nb), F32),
                        pltpu.VMEM((GLA_KEY_WIDTH, nb), F32), pltpu.VMEM((nb, GLA_WIDTH), F32),
                        pltpu.VMEM((nb, GLA_WIDTH), F32), pltpu.VMEM((nb, GLA_WIDTH), F32),
                        pltpu.VMEM((nb, D_MODEL), BF16)],
        compiler_params=pltpu.CompilerParams(dimension_semantics=("arbitrary",), vmem_limit_bytes=VMEM_LIMIT),
        name="mix_sample",
    )(xs, pool2d, s0, g_mix, w_in16, w_fu16, b_f, w_pool16, pool_scale, g_gla, w_out16)


def _query_sample_body(x1_ref, g_mem_ref, w_qm_ref, q_ref):
    q_ref[...] = _mem_query(x1_ref[...], g_mem_ref[...], w_qm_ref) * (MEM_HEAD_DIM ** -0.5)


def _query_sample(x1, g_mem, w_qm16):
    nb = DEC_BATCH
    return pl.pallas_call(
        _query_sample_body,
        grid=(1,),
        in_specs=[_const_spec((nb, D_MODEL)), _const_spec((1, D_MODEL)), _const_spec((D_MODEL, D_MODEL))],
        out_specs=pl.BlockSpec((nb, D_MODEL), lambda i: (0, 0)),
        out_shape=jax.ShapeDtypeStruct((nb, D_MODEL), F32),
        compiler_params=pltpu.CompilerParams(dimension_semantics=("arbitrary",), vmem_limit_bytes=VMEM_LIMIT),
        name="query_sample",
    )(x1, g_mem, w_qm16)


def _attn_sample_body(q_ref, ck_ref, cv_ref, ctx_ref):
    for j in range(SB_ATT):
        s = jnp.sum(ck_ref[j] * q_ref[j][None], axis=-1, keepdims=True)
        e = jnp.exp(s - jnp.max(s, axis=0, keepdims=True))
        p = e / jnp.sum(e, axis=0, keepdims=True)
        ctx_ref[j] = jnp.sum(p * cv_ref[j], axis=0)


def _attn_sample(q3, ck, cv):
    nb = DEC_BATCH
    cache = pl.BlockSpec((SB_ATT, MEM_TOKENS, MEM_HEADS, MEM_HEAD_DIM), lambda i: (i, 0, 0, 0))
    qblk = pl.BlockSpec((SB_ATT, MEM_HEADS, MEM_HEAD_DIM), lambda i: (i, 0, 0))
    return pl.pallas_call(
        _attn_sample_body,
        grid=(nb // SB_ATT,),
        in_specs=[qblk, cache, cache],
        out_specs=qblk,
        out_shape=jax.ShapeDtypeStruct((nb, MEM_HEADS, MEM_HEAD_DIM), F32),
        compiler_params=pltpu.CompilerParams(dimension_semantics=("arbitrary",), vmem_limit_bytes=VMEM_LIMIT),
        name="attn_sample",
    )(q3, ck, cv)


def _ffn_sample_body(x1_ref, ctx_ref, w_om_ref, g_ffn_ref, w_up_ref, w_down_ref, g_final_ref, y_ref):
    y_ref[...] = _ffn_tail(x1_ref[...], ctx_ref[...].astype(BF16), w_om_ref, g_ffn_ref[...], w_up_ref,
                           w_down_ref, g_final_ref[...])


def _ffn_sample(x1, ctx, w_om16, g_ffn, w_up16, w_down16, g_final):
    nb = DEC_BATCH
    return pl.pallas_call(
        _ffn_sample_body,
        grid=(1,),
        in_specs=[_const_spec((nb, D_MODEL)), _const_spec((nb, D_MODEL)), _const_spec((D_MODEL, D_MODEL)),
                  _const_spec((1, D_MODEL)), _const_spec((D_MODEL, D_FF)), _const_spec((D_FF, D_MODEL)),
                  _const_spec((1, D_MODEL))],
        out_specs=pl.BlockSpec((nb, D_MODEL), lambda i: (0, 0)),
        out_shape=jax.ShapeDtypeStruct((nb, D_MODEL), F32),
        compiler_params=pltpu.CompilerParams(dimension_semantics=("arbitrary",), vmem_limit_bytes=VMEM_LIMIT),
        name="ffn_sample",
    )(x1, ctx, w_om16, g_ffn, w_up16, w_down16, g_final)


def kernel(x_prompt, x_sample, state_pool, state_gla, cache_mem_k, cache_mem_v, mem_prompt, norm_mix_g, w_in,
           w_forget_up, b_forget, w_pool, pool_scale, gla_norm_g, w_out, mem_norm_g, w_km, w_vm, norm_mem_g, w_qm,
           w_om, norm_ffn_g, w_up, w_down, norm_final_g):
    assert x_prompt.shape == (BATCH, SEQ, D_MODEL) and x_sample.shape == (DEC_BATCH, 1, D_MODEL)
    assert w_in.shape[0] == 1, "single layer"
    pad_cols = IN_COLS_PAD - w_in.shape[-1]
    w_in16 = jnp.pad(w_in[0], ((0, 0), (0, pad_cols))).astype(BF16)
    w_fu16 = jnp.pad(w_forget_up[0], ((0, LANES - GLA_GATE_RANK), (0, 0))).astype(BF16)
    w_pool16 = w_pool[0].astype(BF16)
    w_out16 = w_out[0].astype(BF16)
    w_kv16 = jnp.concatenate([w_km[0], w_vm[0]], axis=1).astype(BF16)
    w_qm16 = w_qm[0].astype(BF16)
    w_om16 = w_om[0].astype(BF16)
    w_up16 = w_up[0].astype(BF16)
    w_down16 = w_down[0].astype(BF16)
    g_mix = norm_mix_g.reshape(1, D_MODEL)
    g_memn = mem_norm_g.reshape(1, D_MODEL)
    g_mem = norm_mem_g.reshape(1, D_MODEL)
    g_ffn = norm_ffn_g.reshape(1, D_MODEL)
    g_final = norm_final_g.reshape(1, D_MODEL)
    g_gla = gla_norm_g.reshape(1, GLA_WIDTH)
    b_f = b_forget.reshape(1, GLA_KEY_WIDTH)
    p_scale = pool_scale.reshape(1, POOL_WIDTH)

    mk, mv, mk16, mv16 = _mem_kv(mem_prompt.reshape(BATCH * MEM_TOKENS, D_MODEL), g_memn, w_kv16)
    x1p, tail_p, gla_p = _mix_prompt(x_prompt.reshape(BATCH * SEQ, D_MODEL), g_mix, w_in16, w_fu16, b_f, w_pool16,
                                     p_scale, g_gla, w_out16)
    y_p = _tail_prompt(x1p, mk16, mv16, g_mem, w_qm16, w_om16, g_ffn, w_up16, w_down16, g_final)

    x1s, pool_s, gla_s = _mix_sample(x_sample.reshape(DEC_BATCH, D_MODEL),
                                     state_pool.reshape(DEC_BATCH, POOL_BUF * POOL_WIDTH),
                                     state_gla.reshape(DEC_BATCH, GLA_KEY_WIDTH, GLA_DV), g_mix, w_in16, w_fu16,
                                     b_f, w_pool16, p_scale, g_gla, w_out16)
    cache_shape = (DEC_BATCH, MEM_TOKENS, MEM_HEADS, MEM_HEAD_DIM)
    q_s = _query_sample(x1s, g_mem, w_qm16).reshape(DEC_BATCH, MEM_HEADS, MEM_HEAD_DIM)
    ctx_s = _attn_sample(q_s, cache_mem_k.reshape(cache_shape), cache_mem_v.reshape(cache_shape))
    y_s = _ffn_sample(x1s, ctx_s.reshape(DEC_BATCH, D_MODEL), w_om16, g_ffn, w_up16, w_down16, g_final)

    kv_shape = (1, BATCH, MEM_TOKENS, MEM_HEADS, MEM_HEAD_DIM)
    return (y_p.reshape(BATCH, SEQ, D_MODEL),
            y_s.reshape(DEC_BATCH, 1, D_MODEL),
            tail_p[:, POOL_TAIL - POOL_BUF:, :].reshape(1, BATCH, POOL_BUF, POOL_WIDTH),
            gla_p.reshape(1, BATCH, GLA_HEADS, GLA_DK, GLA_DV),
            mk.reshape(kv_shape),
            mv.reshape(kv_shape),
            pool_s.reshape(1, DEC_BATCH, POOL_BUF, POOL_WIDTH),
            gla_s.reshape(1, DEC_BATCH, GLA_HEADS, GLA_DK, GLA_DV))
```
